```python
import math
import jax, jax.numpy as jnp
from jax import lax
import numpy as np

D_MODEL = 1024
BATCH = 8
SEQ = 4096
DEPTH = 1

DIFF_QK_DIM = 64
DIFF_V_DIM = 2 * DIFF_QK_DIM
N_DIFF_HEADS = (D_MODEL // 2) // DIFF_V_DIM
FOX_HEAD_DIM = 64
N_FOX_HEADS = (D_MODEL // 2) // FOX_HEAD_DIM
DIFF_QK_W = N_DIFF_HEADS * 2 * DIFF_QK_DIM
DIFF_V_W = N_DIFF_HEADS * DIFF_V_DIM
FOX_W = N_FOX_HEADS * FOX_HEAD_DIM
MIX_W = DIFF_V_W + FOX_W
IN_W = 2 * DIFF_QK_W + DIFF_V_W + 3 * FOX_W + N_FOX_HEADS

N_MEM = 256
N_CROSS_HEADS = 4
CROSS_HEAD_DIM = D_MODEL // N_CROSS_HEADS
D_FF = 4 * D_MODEL
ROPE_THETA = 500000.0
ROT_DIM = DIFF_QK_DIM // 4
Q_BLOCK = 128
EPS = 1e-6
SUBLN_EPS = 1e-5

kernel_name = "hymba_diff_fox_hybrid_layer"


def rmsnorm(x, g, eps=EPS):
    xf = x.astype(jnp.float32)
    y = xf * lax.rsqrt(jnp.mean(xf * xf, axis=-1, keepdims=True) + eps)
    return (y * g.astype(jnp.float32)).astype(x.dtype)


def rope_tables(seq):
    pos = jnp.arange(seq, dtype=jnp.float32)
    inv_freq = ROPE_THETA ** (-jnp.arange(0, ROT_DIM, 2, dtype=jnp.float32) / ROT_DIM)
    ang = pos[:, None] * inv_freq[None, :]
    return jnp.cos(ang), jnp.sin(ang)


def apply_partial_rope(x, cos, sin):
    half = ROT_DIM // 2
    c = cos.astype(x.dtype)
    s = sin.astype(x.dtype)
    x1 = x[..., :half]
    x2 = x[..., half:ROT_DIM]
    return jnp.concatenate([x1 * c - x2 * s, x2 * c + x1 * s, x[..., ROT_DIM:]], axis=-1)


def causal_block_mask(start, end):
    qpos = start + jnp.arange(Q_BLOCK)[:, None]
    kpos = jnp.arange(end)[None, :]
    return kpos <= qpos


def diff_attention(q, k, v, lam):
    seq = q.shape[3]
    scale = DIFF_QK_DIM ** -0.5
    neg = jnp.finfo(jnp.float32).min
    outs = []
    for start in range(0, seq, Q_BLOCK):
        end = start + Q_BLOCK
        s = jnp.einsum('bhmqd,bhmkd->bhmqk', q[:, :, :, start:end], k[:, :, :, :end]).astype(jnp.float32) * scale
        s = jnp.where(causal_block_mask(start, end), s, neg)
        p = jax.nn.softmax(s, axis=-1)
        a = p[:, :, 0] - lam * p[:, :, 1]
        outs.append(jnp.einsum('bhqk,bhkd->bhqd', a.astype(v.dtype), v[:, :, :end]))
    return jnp.concatenate(outs, axis=2)


def forgetting_attention(q, k, v, log_f):
    seq = q.shape[2]
    scale = FOX_HEAD_DIM ** -0.5
    neg = jnp.finfo(jnp.float32).min
    c = jnp.cumsum(log_f, axis=-1)
    outs = []
    for start in range(0, seq, Q_BLOCK):
        end = start + Q_BLOCK
        s = jnp.einsum('bhqd,bhkd->bhqk', q[:, :, start:end], k[:, :, :end]).astype(jnp.float32) * scale
        s = s + c[:, :, start:end, None] - c[:, :, None, :end]
        s = jnp.where(causal_block_mask(start, end), s, neg)
        p = jax.nn.softmax(s, axis=-1)
        outs.append(jnp.einsum('bhqk,bhkd->bhqd', p.astype(v.dtype), v[:, :, :end]))
    return jnp.concatenate(outs, axis=2)


def setup_inputs(seed: int = 0) -> dict:
    key = jax.random.key(seed)
    ks = jax.random.split(key, 24)
    f32 = jnp.float32
    nrm = lambda k, shape, scale: jax.random.normal(k, shape, f32) * scale
    gain = lambda k, shape: 1.0 + 0.02 * jax.random.normal(k, shape, f32)
    return {
        "x": jax.random.normal(ks[0], (BATCH, SEQ, D_MODEL), f32),
        "mem": jax.random.normal(ks[1], (BATCH, N_MEM, D_MODEL), f32),
        "norm_mix_g": gain(ks[2], (DEPTH, D_MODEL)),
        "w_in": nrm(ks[3], (DEPTH, D_MODEL, IN_W), D_MODEL ** -0.5),
        "b_forget": 1.0 + 0.3 * jax.random.normal(ks[4], (DEPTH, N_FOX_HEADS), f32),
        "lam_q1": nrm(ks[5], (DEPTH, DIFF_QK_DIM), 0.1),
        "lam_k1": nrm(ks[6], (DEPTH, DIFF_QK_DIM), 0.1),
        "lam_q2": nrm(ks[7], (DEPTH, DIFF_QK_DIM), 0.1),
        "lam_k2": nrm(ks[8], (DEPTH, DIFF_QK_DIM), 0.1),
        "diff_subln_g": gain(ks[9], (DEPTH, DIFF_V_DIM)),
        "fox_out_g": gain(ks[10], (DEPTH, FOX_HEAD_DIM)),
        "w_out": nrm(ks[11], (DEPTH, MIX_W, D_MODEL), MIX_W ** -0.5),
        "norm_cross_g": gain(ks[12], (DEPTH, D_MODEL)),
        "norm_mem_g": gain(ks[13], (DEPTH, D_MODEL)),
        "w_cq": nrm(ks[14], (DEPTH, D_MODEL, D_MODEL), D_MODEL ** -0.5),
        "w_ckv": nrm(ks[15], (DEPTH, D_MODEL, 2 * D_MODEL), D_MODEL ** -0.5),
        "w_co": nrm(ks[16], (DEPTH, D_MODEL, D_MODEL), D_MODEL ** -0.5),
        "norm_mlp_g": gain(ks[17], (DEPTH, D_MODEL)),
        "w_up": nrm(ks[18], (DEPTH, D_MODEL, D_FF), D_MODEL ** -0.5),
        "w_down": nrm(ks[19], (DEPTH, D_FF, D_MODEL), D_FF ** -0.5),
        "norm_final_g": gain(ks[20], (D_MODEL,)),
    }


def reference(x, mem, norm_mix_g, w_in, b_forget, lam_q1, lam_k1, lam_q2, lam_k2,
              diff_subln_g, fox_out_g, w_out, norm_cross_g, norm_mem_g, w_cq, w_ckv, w_co,
              norm_mlp_g, w_up, w_down, norm_final_g):
    B, S, D = x.shape
    M = mem.shape[1]
    cos, sin = rope_tables(S)
    cos = cos[:, None, None, :]
    sin = sin[:, None, None, :]
    split_pts = [DIFF_QK_W, 2 * DIFF_QK_W, 2 * DIFF_QK_W + DIFF_V_W,
                 2 * DIFF_QK_W + DIFF_V_W + FOX_W, 2 * DIFF_QK_W + DIFF_V_W + 2 * FOX_W,
                 2 * DIFF_QK_W + DIFF_V_W + 3 * FOX_W]
    h = x
    for l in range(DEPTH):
        u = rmsnorm(h, norm_mix_g[l])
        proj = u @ w_in[l]
        dq, dk, dv, fq, fk, fv, fgate = jnp.split(proj, split_pts, axis=-1)

        lambda_init = 0.8 - 0.6 * math.exp(-0.3 * l)
        lam = (jnp.exp(jnp.sum(lam_q1[l].astype(jnp.float32) * lam_k1[l].astype(jnp.float32)))
               - jnp.exp(jnp.sum(lam_q2[l].astype(jnp.float32) * lam_k2[l].astype(jnp.float32)))
               + lambda_init)
        dq = apply_partial_rope(dq.reshape(B, S, N_DIFF_HEADS, 2, DIFF_QK_DIM), cos, sin)
        dk = apply_partial_rope(dk.reshape(B, S, N_DIFF_HEADS, 2, DIFF_QK_DIM), cos, sin)
        dv = dv.reshape(B, S, N_DIFF_HEADS, DIFF_V_DIM).transpose(0, 2, 1, 3)
        d_out = diff_attention(dq.transpose(0, 2, 3, 1, 4), dk.transpose(0, 2, 3, 1, 4), dv, lam)
        d_out = rmsnorm(d_out, diff_subln_g[l], SUBLN_EPS) * (1.0 - lambda_init)
        d_out = d_out.transpose(0, 2, 1, 3).reshape(B, S, DIFF_V_W)

        log_f = jax.nn.log_sigmoid((fgate + b_forget[l]).astype(jnp.float32)).transpose(0, 2, 1)
        to_heads = lambda t: t.reshape(B, S, N_FOX_HEADS, FOX_HEAD_DIM).transpose(0, 2, 1, 3)
        f_out = forgetting_attention(to_heads(fq), to_heads(fk), to_heads(fv), log_f)
        f_out = rmsnorm(f_out, fox_out_g[l])
        f_out = f_out.transpose(0, 2, 1, 3).reshape(B, S, FOX_W)

        h = h + jnp.concatenate([d_out, f_out], axis=-1) @ w_out[l]

        cq = (rmsnorm(h, norm_cross_g[l]) @ w_cq[l]).reshape(B, S, N_CROSS_HEADS, CROSS_HEAD_DIM)
        ckv = rmsnorm(mem, norm_mem_g[l]) @ w_ckv[l]
        ck, cv = jnp.split(ckv, 2, axis=-1)
        ck = ck.reshape(B, M, N_CROSS_HEADS, CROSS_HEAD_DIM)
        cv = cv.reshape(B, M, N_CROSS_HEADS, CROSS_HEAD_DIM)
        cs = jnp.einsum('bshd,bmhd->bhsm', cq, ck).astype(jnp.float32) * (CROSS_HEAD_DIM ** -0.5)
        cp = jax.nn.softmax(cs, axis=-1).astype(cv.dtype)
        co = jnp.einsum('bhsm,bmhd->bshd', cp, cv).reshape(B, S, D)
        h = h + co @ w_co[l]

        z = rmsnorm(h, norm_mlp_g[l]) @ w_up[l]
        h = h + jnp.square(jax.nn.relu(z)) @ w_down[l]
    return rmsnorm(h, norm_final_g)
```

```python
import functools

import jax
import jax.numpy as jnp
from jax import lax
from jax.experimental import pallas as pl
from jax.experimental.pallas import tpu as pltpu

F32 = jnp.float32
BF16 = jnp.bfloat16

LANES = 128
QK_DIM = 64
ROT_DIM = 16
ROPE_THETA = 500000.0
N_DIFF_HEADS = 4
N_FOX_HEADS = 8
N_CROSS_HEADS = 4
EPS = 1e-6
SUBLN_EPS = 1e-5
LAMBDA_INIT = 0.8 - 0.6 * 1.0
NEG = -1e30

VMEM_LIMIT = 56 * 1024 * 1024


def _rms(x, g, eps):
    ms = jnp.mean(x * x, axis=-1, keepdims=True)
    return x * lax.rsqrt(ms + eps) * g


def _inproj_kernel(x_ref, g_ref, w_ref, wg_ref, bg_ref, rc_ref, ra_ref, rb_ref,
                   qkv_ref, ccol_ref, crow_ref, carry_ref, *, tiles_per_batch, tm):
    i = pl.program_id(0)
    u = _rms(x_ref[...], g_ref[...], EPS).astype(BF16)
    rc, ra, rb = rc_ref[...], ra_ref[...], rb_ref[...]
    scale = QK_DIM ** -0.5
    for c in range(6):
        y = jnp.dot(u, w_ref[:, c * 512:(c + 1) * 512], preferred_element_type=F32)
        for j in range(4):
            t = y[:, j * LANES:(j + 1) * LANES]
            if c in (0, 1):
                t = t * rc + pltpu.roll(t, 8, 1) * ra + pltpu.roll(t, LANES - 8, 1) * rb
            if c in (0, 3):
                t = t * scale
            qkv_ref[c * 4 + j] = t.astype(BF16)

    gate = jnp.dot(u, wg_ref[...], preferred_element_type=F32) + bg_ref[...]
    logf = jnp.minimum(gate, 0.0) - jnp.log1p(jnp.exp(-jnp.abs(gate)))

    @pl.when(i % tiles_per_batch == 0)
    def _():
        carry_ref[...] = jnp.zeros_like(carry_ref)

    sub = 256
    row = lax.broadcasted_iota(jnp.int32, (sub, sub), 0)
    col = lax.broadcasted_iota(jnp.int32, (sub, sub), 1)
    tri = (row >= col).astype(BF16)
    carry = carry_ref[0:1, :]
    parts = []
    for r in range(tm // sub):
        blk = logf[r * sub:(r + 1) * sub]
        hi = blk.astype(BF16)
        r1 = blk - hi.astype(F32)
        mid = r1.astype(BF16)
        lo = (r1 - mid.astype(F32)).astype(BF16)
        cs = (jnp.dot(tri, hi, preferred_element_type=F32)
              + jnp.dot(tri, mid, preferred_element_type=F32)
              + jnp.dot(tri, lo, preferred_element_type=F32)) + carry
        carry = cs[sub - 1:sub, :]
        parts.append(cs)
    carry_ref[...] = jnp.broadcast_to(carry, carry_ref.shape)
    cs_all = jnp.concatenate(parts, axis=0)
    ccol_ref[...] = cs_all
    crow_ref[...] = cs_all.T[:N_FOX_HEADS, :]


def _inproj(x2, g, w_main, w_gate, b_gate, rc, ra, rb, *, batch, seq, tm):
    rows, d = x2.shape
    tiles_per_batch = seq // tm
    n_blocks = w_main.shape[1] // LANES
    const = lambda i: (0, 0)
    return pl.pallas_call(
        functools.partial(_inproj_kernel, tiles_per_batch=tiles_per_batch, tm=tm),
        grid=(rows // tm,),
        in_specs=[
            pl.BlockSpec((tm, d), lambda i: (i, 0)),
            pl.BlockSpec((1, d), const),
            pl.BlockSpec(w_main.shape, const),
            pl.BlockSpec(w_gate.shape, const),
            pl.BlockSpec((1, LANES), const),
            pl.BlockSpec((tm, LANES), lambda i: (i % tiles_per_batch, 0)),
            pl.BlockSpec((tm, LANES), lambda i: (i % tiles_per_batch, 0)),
            pl.BlockSpec((tm, LANES), lambda i: (i % tiles_per_batch, 0)),
        ],
        out_specs=[
            pl.BlockSpec((None, n_blocks, tm, LANES),
                         lambda i: (i // tiles_per_batch, 0, i % tiles_per_batch, 0)),
            pl.BlockSpec((tm, LANES), lambda i: (i, 0)),
            pl.BlockSpec((None, N_FOX_HEADS, tm),
                         lambda i: (i // tiles_per_batch, 0, i % tiles_per_batch)),
        ],
        out_shape=[
            jax.ShapeDtypeStruct((batch, n_blocks, seq, LANES), BF16),
            jax.ShapeDtypeStruct((rows, LANES), F32),
            jax.ShapeDtypeStruct((batch, N_FOX_HEADS, seq), F32),
        ],
        scratch_shapes=[pltpu.VMEM((8, LANES), F32)],
        compiler_params=pltpu.CompilerParams(
            dimension_semantics=("arbitrary",), vmem_limit_bytes=VMEM_LIMIT),
        name="inproj",
    )(x2, g, w_main, w_gate, b_gate, rc, ra, rb)


def _attn_tile_update(qq, k, v, bias_rows, mask, m_sc, l_sc, acc_sc, tq, tk):
    s = lax.dot_general(qq, k, (((1,), (1,)), ((), ())), preferred_element_type=F32)
    ps, alphas = [], []
    for hh in range(2):
        sh = s[hh * tq:(hh + 1) * tq]
        if bias_rows is not None:
            sh = sh - bias_rows[hh]
        if mask is not None:
            sh = jnp.where(mask, sh, NEG)
        m_prev = m_sc[hh]
        m_next = jnp.maximum(m_prev, jnp.max(sh, axis=1, keepdims=True))
        p = jnp.exp(sh - jnp.tile(m_next, (1, tk // LANES)))
        alpha = jnp.exp(m_prev - m_next)
        l_sc[hh] = alpha * l_sc[hh] + jnp.sum(p, axis=1, keepdims=True)
        m_sc[hh] = m_next
        ps.append(p.astype(BF16))
        alphas.append(alpha)
    pv = jnp.dot(jnp.concatenate(ps, axis=0), v, preferred_element_type=F32)
    for hh in range(2):
        acc_sc[hh] = acc_sc[hh] * alphas[hh] + pv[hh * tq:(hh + 1) * tq]


def _attn_kernel(*refs, kind, tq, seq):
    if kind == "diff":
        lam_ref, g_ref, q_ref, k_ref, v_ref, o_ref, m_sc, l_sc, acc_sc = refs
        c_ref = None
    else:
        g_ref, cc_ref, c_ref, q_ref, k_ref, v_ref, o_ref, m_sc, l_sc, acc_sc = refs
    tk = tq
    lane = lax.broadcasted_iota(jnp.int32, (tq, LANES), 1)
    low = lane < QK_DIM
    row = lax.broadcasted_iota(jnp.int32, (tq, tk), 0)
    col = lax.broadcasted_iota(jnp.int32, (tq, tk), 1)
    causal = col <= row

    def q_body(qi, carry):
        q0 = pl.multiple_of(qi * tq, tq)
        q = q_ref[pl.ds(q0, tq), :]
        zero = jnp.zeros_like(q)
        qq = jnp.concatenate([jnp.where(low, q, zero), jnp.where(low, zero, q)], axis=0)
        m_sc[...] = jnp.full(m_sc.shape, NEG, F32)
        l_sc[...] = jnp.zeros(l_sc.shape, F32)
        acc_sc[...] = jnp.zeros(acc_sc.shape, F32)

        if c_ref is not None:
            cblk = cc_ref[pl.ds(q0, tq), :]
            head0 = 2 * pl.program_id(1)
            cq_cols = [jnp.sum(jnp.where(lane == head0 + hh, cblk, 0.0), axis=1, keepdims=True)
                       for hh in range(2)]

        def kv_step(k0, mask):
            k = k_ref[pl.ds(k0, tk), :]
            v = v_ref[pl.ds(k0, tk), :]
            bias_rows = None
            if c_ref is not None:
                bias_rows = [c_ref[hh:hh + 1, pl.ds(k0, tk)] - cq_cols[hh] for hh in range(2)]
            _attn_tile_update(qq, k, v, bias_rows, mask, m_sc, l_sc, acc_sc, tq, tk)

        def kv_body(kj, c):
            kv_step(pl.multiple_of(kj * tk, tk), None)
            return c

        lax.fori_loop(0, qi, kv_body, 0)
        kv_step(q0, causal)

        o0 = acc_sc[0] / l_sc[0]
        o1 = acc_sc[1] / l_sc[1]
        if kind == "diff":
            lam = (jnp.exp(jnp.sum(lam_ref[0:1, :] * lam_ref[1:2, :], axis=1, keepdims=True))
                   - jnp.exp(jnp.sum(lam_ref[2:3, :] * lam_ref[3:4, :], axis=1, keepdims=True))
                   + LAMBDA_INIT)
            o = o0 - lam * o1
            o = _rms(o, g_ref[...], SUBLN_EPS) * (1.0 - LAMBDA_INIT)
        else:
            o = jnp.where(low, o0, o1)
            sq = o * o
            ss_all = jnp.sum(sq, axis=1, keepdims=True)
            ss_low = jnp.sum(jnp.where(low, sq, 0.0), axis=1, keepdims=True)
            ms = jnp.where(low, ss_low, ss_all - ss_low) * (1.0 / QK_DIM)
            o = o * lax.rsqrt(ms + EPS) * g_ref[...]
        o_ref[pl.ds(q0, tq), :] = o.astype(o_ref.dtype)
        return carry

    lax.fori_loop(0, seq // tq, q_body, 0)


def _attention(kind, qkv, extra, *, tq):
    batch, _, seq, _ = qkv.shape
    n_groups = 4
    base = 0 if kind == "diff" else 12
    blk = lambda off: pl.BlockSpec((None, None, seq, LANES),
                                   lambda b, h, off=off: (b, base + off + h, 0, 0))
    if kind == "diff":
        lam4, g = extra
        extra_specs = [pl.BlockSpec(lam4.shape, lambda b, h: (0, 0)),
                       pl.BlockSpec(g.shape, lambda b, h: (0, 0))]
        extra_args = [lam4, g]
    else:
        g, ccol, crow = extra
        extra_specs = [pl.BlockSpec(g.shape, lambda b, h: (0, 0)),
                       pl.BlockSpec((None, seq, LANES), lambda b, h: (b, 0, 0)),
                       pl.BlockSpec((None, None, 2, seq), lambda b, h: (b, h, 0, 0))]
        extra_args = [g, ccol, crow]
    return pl.pallas_call(
        functools.partial(_attn_kernel, kind=kind, tq=tq, seq=seq),
        grid=(batch, n_groups),
        in_specs=extra_specs + [blk(0), blk(4), blk(8)],
        out_specs=pl.BlockSpec((None, seq, LANES), lambda b, h: (b, 0, h)),
        out_shape=jax.ShapeDtypeStruct((batch, seq, n_groups * LANES), BF16),
        scratch_shapes=[pltpu.VMEM((2, tq, LANES), F32)] * 3,
        compiler_params=pltpu.CompilerParams(
            dimension_semantics=("arbitrary", "arbitrary"), vmem_limit_bytes=VMEM_LIMIT),
        name=kind + "_attn",
    )(*extra_args, qkv, qkv, qkv)


def _memkv_kernel(mem_ref, g_ref, w_ref, kv_ref):
    u = _rms(mem_ref[...], g_ref[...], EPS).astype(BF16)
    kv_ref[...] = jnp.dot(u, w_ref[...], preferred_element_type=F32).astype(BF16)


def _memkv(mem2, g, w, *, tm):
    rows, d = mem2.shape
    n = w.shape[1]
    return pl.pallas_call(
        _memkv_kernel,
        grid=(rows // tm,),
        in_specs=[pl.BlockSpec((tm, d), lambda i: (i, 0)),
                  pl.BlockSpec((1, d), lambda i: (0, 0)),
                  pl.BlockSpec(w.shape, lambda i: (0, 0))],
        out_specs=pl.BlockSpec((tm, n), lambda i: (i, 0)),
        out_shape=jax.ShapeDtypeStruct((rows, n), BF16),
        compiler_params=pltpu.CompilerParams(
            dimension_semantics=("arbitrary",), vmem_limit_bytes=VMEM_LIMIT),
        name="memkv",
    )(mem2, g, w)


def _cross_kernel(x_ref, d_ref, f_ref, wo_ref, g_ref, wq_ref, kv_ref, wco_ref, h_ref):
    half = d_ref.shape[1]
    h1 = (x_ref[...]
          + jnp.dot(d_ref[...], wo_ref[:half, :], preferred_element_type=F32)
          + jnp.dot(f_ref[...], wo_ref[half:, :], preferred_element_type=F32))
    uq = _rms(h1, g_ref[...], EPS).astype(BF16)
    dm = h1.shape[1]
    hd = dm // N_CROSS_HEADS
    cq = (jnp.dot(uq, wq_ref[...], preferred_element_type=F32) * (hd ** -0.5)).astype(BF16)
    outs = []
    for hh in range(N_CROSS_HEADS):
        qh = cq[:, hh * hd:(hh + 1) * hd]
        kh = kv_ref[:, hh * hd:(hh + 1) * hd]
        vh = kv_ref[:, dm + hh * hd:dm + (hh + 1) * hd]
        s = lax.dot_general(qh, kh, (((1,), (1,)), ((), ())), preferred_element_type=F32)
        p = jnp.exp(s - jnp.max(s, axis=1, keepdims=True))
        l = jnp.sum(p, axis=1, keepdims=True)
        oh = jnp.dot(p.astype(BF16), vh, preferred_element_type=F32) / l
        outs.append(oh.astype(BF16))
    co = jnp.concatenate(outs, axis=1)
    h_ref[...] = h1 + jnp.dot(co, wco_ref[...], preferred_element_type=F32)


def _cross(x2, d2, f2, w_out, g, w_cq, kv, w_co, *, seq, tm):
    rows, dm = x2.shape
    tiles_per_batch = seq // tm
    n_mem = kv.shape[1]
    const = lambda i: (0, 0)
    return pl.pallas_call(
        _cross_kernel,
        grid=(rows // tm,),
        in_specs=[
            pl.BlockSpec((tm, dm), lambda i: (i, 0)),
            pl.BlockSpec((tm, d2.shape[1]), lambda i: (i, 0)),
            pl.BlockSpec((tm, f2.shape[1]), lambda i: (i, 0)),
            pl.BlockSpec(w_out.shape, const),
            pl.BlockSpec((1, dm), const),
            pl.BlockSpec(w_cq.shape, const),
            pl.BlockSpec((None, n_mem, kv.shape[2]), lambda i: (i // tiles_per_batch, 0, 0)),
            pl.BlockSpec(w_co.shape, const),
        ],
        out_specs=pl.BlockSpec((tm, dm), lambda i: (i, 0)),
        out_shape=jax.ShapeDtypeStruct((rows, dm), F32),
        compiler_params=pltpu.CompilerParams(
            dimension_semantics=("arbitrary",), vmem_limit_bytes=VMEM_LIMIT),
        name="cross",
    )(x2, d2, f2, w_out, g, w_cq, kv, w_co)


def _mlp_kernel(h_ref, g_ref, wu_ref, wd_ref, gf_ref, o_ref, *, fc):
    h = h_ref[...]
    u = _rms(h, g_ref[...], EPS).astype(BF16)
    acc = h
    for c in range(wu_ref.shape[1] // fc):
        z = jnp.dot(u, wu_ref[:, c * fc:(c + 1) * fc], preferred_element_type=F32)
        a = jnp.square(jnp.maximum(z, 0.0)).astype(BF16)
        acc = acc + jnp.dot(a, wd_ref[c * fc:(c + 1) * fc, :], preferred_element_type=F32)
    o_ref[...] = _rms(acc, gf_ref[...], EPS)


def _mlp(h2, g, w_up, w_down, g_final, *, tm, fc):
    rows, dm = h2.shape
    const = lambda i: (0, 0)
    return pl.pallas_call(
        functools.partial(_mlp_kernel, fc=fc),
        grid=(rows // tm,),
        in_specs=[
            pl.BlockSpec((tm, dm), lambda i: (i, 0)),
            pl.BlockSpec((1, dm), const),
            pl.BlockSpec(w_up.shape, const, pipeline_mode=pl.Buffered(1)),
            pl.BlockSpec(w_down.shape, const, pipeline_mode=pl.Buffered(1)),
            pl.BlockSpec((1, dm), const),
        ],
        out_specs=pl.BlockSpec((tm, dm), lambda i: (i, 0)),
        out_shape=jax.ShapeDtypeStruct((rows, dm), F32),
        compiler_params=pltpu.CompilerParams(
            dimension_semantics=("arbitrary",), vmem_limit_bytes=VMEM_LIMIT),
        name="mlp",
    )(h2, g, w_up, w_down, g_final)


def _rope_lane_tables(seq):
    pos = jnp.arange(seq, dtype=F32)
    inv_freq = ROPE_THETA ** (-jnp.arange(0, ROT_DIM, 2, dtype=F32) / ROT_DIM)
    ang = pos[:, None] * inv_freq[None, :]
    cos, sin = jnp.cos(ang), jnp.sin(ang)
    lane = jnp.arange(LANES)
    l64 = lane % QK_DIM
    fidx = l64 % (ROT_DIM // 2)
    in_rot = (l64 < ROT_DIM)[None, :]
    first = (l64 < ROT_DIM // 2)[None, :]
    rc = jnp.where(in_rot, cos[:, fidx], 1.0)
    ra = jnp.where(in_rot & ~first, sin[:, fidx], 0.0)
    rb = jnp.where(first, -sin[:, fidx], 0.0)
    return rc.astype(F32), ra.astype(F32), rb.astype(F32)


def kernel(x, mem, norm_mix_g, w_in, b_forget, lam_q1, lam_k1, lam_q2, lam_k2,
           diff_subln_g, fox_out_g, w_out, norm_cross_g, norm_mem_g, w_cq, w_ckv, w_co,
           norm_mlp_g, w_up, w_down, norm_final_g):
    batch, seq, dm = x.shape
    n_mem = mem.shape[1]
    assert norm_mix_g.shape[0] == 1, "single-layer block"
    main_w = 6 * 512
    assert w_in.shape[2] == main_w + N_FOX_HEADS

    x2 = x.reshape(batch * seq, dm)
    w_main = w_in[0, :, :main_w].astype(BF16)
    w_gate = jnp.pad(w_in[0, :, main_w:], ((0, 0), (0, LANES - N_FOX_HEADS))).astype(BF16)
    b_gate = jnp.pad(b_forget[0], (0, LANES - N_FOX_HEADS)).reshape(1, LANES)
    rc, ra, rb = _rope_lane_tables(seq)

    qkv, ccol, crow = _inproj(x2, norm_mix_g[0].reshape(1, dm), w_main, w_gate, b_gate, rc, ra, rb,
                        batch=batch, seq=seq, tm=512)

    lam4 = jnp.stack([lam_q1[0], lam_k1[0], lam_q2[0], lam_k2[0]], axis=0)
    d_out = _attention("diff", qkv, (lam4, diff_subln_g[0].reshape(1, LANES)), tq=256)
    fox_g = jnp.tile(fox_out_g[0], 2).reshape(1, LANES)
    f_out = _attention("fox", qkv, (fox_g, ccol.reshape(batch, seq, LANES),
                                    crow.reshape(batch, 4, 2, seq)), tq=256)

    kv = _memkv(mem.reshape(batch * n_mem, dm), norm_mem_g[0].reshape(1, dm),
                w_ckv[0].astype(BF16), tm=512)
    h2 = _cross(x2, d_out.reshape(batch * seq, -1), f_out.reshape(batch * seq, -1),
                w_out[0].astype(BF16), norm_cross_g[0].reshape(1, dm), w_cq[0].astype(BF16),
                kv.reshape(batch, n_mem, -1), w_co[0].astype(BF16), seq=seq, tm=512)
    out = _mlp(h2, norm_mlp_g[0].reshape(1, dm), w_up[0].astype(BF16), w_down[0].astype(BF16),
               norm_final_g.reshape(1, dm), tm=512, fc=1024)
    return out.reshape(batch, seq, dm)
```

```python
import functools

import jax
import jax.numpy as jnp
from jax import lax
from jax.experimental import pallas as pl
from jax.experimental.pallas import tpu as pltpu

F32 = jnp.float32
BF16 = jnp.bfloat16

LANES = 128
SUBLANES = 8
MXU_COLS = 256
QK_DIM = 64
ROT_DIM = 16
ROPE_THETA = 500000.0
N_DIFF_HEADS = 4
N_FOX_HEADS = 8
N_FOX_PAIRS = N_FOX_HEADS // 2
N_CROSS_HEADS = 4
EPS = 1e-6
SUBLN_EPS = 1e-5
LAMBDA_INIT = 0.8 - 0.6 * 1.0
NEG = -1e30
LOG2E = 1.4426950408889634

VMEM_LIMIT = 56 * 1024 * 1024

ROW_TILE = 512
ATTN_TQ = 512
ATTN_TK = 256

QT_BLOCKS = N_DIFF_HEADS + N_FOX_HEADS
K_BLOCKS = N_DIFF_HEADS + N_FOX_HEADS
VT_BLOCKS = N_DIFF_HEADS + N_FOX_PAIRS


def _rms(x, g, eps):
    ms = jnp.mean(x * x, axis=-1, keepdims=True)
    return x * lax.rsqrt(ms + eps) * g


def _bf16_pieces(c):
    hi = c.astype(BF16).astype(F32)
    r1 = c - hi
    mid = r1.astype(BF16).astype(F32)
    lo = (r1 - mid).astype(BF16).astype(F32)
    return hi, mid, lo


def _inproj_kernel(x_ref, g_ref, wn_ref, wt_ref, wg_ref, bg_ref, rc_ref, ra_ref, rb_ref,
                   ct_ref, st_ref, qt_ref, k_ref, vt_ref, carry_ref, *, tiles_per_batch, tm):
    i = pl.program_id(0)
    u = _rms(x_ref[...], g_ref[...], EPS).astype(BF16)
    scale = QK_DIM ** -0.5 * LOG2E

    gate = jnp.dot(u, wg_ref[...], preferred_element_type=F32) + bg_ref[...]
    logf = jnp.minimum(gate, 0.0) - jnp.log1p(jnp.exp(-jnp.abs(gate)))

    @pl.when(i % tiles_per_batch == 0)
    def _():
        carry_ref[...] = jnp.zeros_like(carry_ref)

    sub = 256
    row = lax.broadcasted_iota(jnp.int32, (sub, sub), 0)
    col = lax.broadcasted_iota(jnp.int32, (sub, sub), 1)
    tri = (row >= col).astype(BF16)
    carry = carry_ref[0:1, :]
    parts = []
    for r in range(tm // sub):
        hi, mid, lo = _bf16_pieces(logf[r * sub:(r + 1) * sub])
        cs = (jnp.dot(tri, hi.astype(BF16), preferred_element_type=F32)
              + jnp.dot(tri, mid.astype(BF16), preferred_element_type=F32)
              + jnp.dot(tri, lo.astype(BF16), preferred_element_type=F32)) + carry
        carry = cs[sub - 1:sub, :]
        parts.append(cs)
    carry_ref[...] = jnp.broadcast_to(carry, carry_ref.shape)
    cs = jnp.concatenate(parts, axis=0) * LOG2E
    cst = cs.T

    rc, ra, rb = rc_ref[...], ra_ref[...], rb_ref[...]
    lane = lax.broadcasted_iota(jnp.int32, (tm, LANES), 1)
    low = lane < QK_DIM
    l64 = lane % QK_DIM
    yn = jnp.dot(u, wn_ref[...], preferred_element_type=F32)
    for h in range(N_DIFF_HEADS):
        t = yn[:, h * LANES:(h + 1) * LANES]
        t = t * rc + pltpu.roll(t, 8, 1) * ra + pltpu.roll(t, LANES - 8, 1) * rb
        k_ref[h] = t.astype(BF16)
    for p in range(N_FOX_PAIRS):
        yp = yn[:, (N_DIFF_HEADS + p) * LANES:(N_DIFF_HEADS + p + 1) * LANES]
        rolled = cs if p == 0 else pltpu.roll(cs, LANES - 8 * p, 1)
        hi, mid, lo = _bf16_pieces(rolled)
        augk = jnp.where(l64 == 0, -hi, jnp.where(l64 == 1, -mid, jnp.where(
            l64 == 2, -lo, jnp.where(l64 < 6, 1.0, 0.0))))
        k_ref[N_DIFF_HEADS + 2 * p] = jnp.where(low, yp, augk).astype(BF16)
        k_ref[N_DIFF_HEADS + 2 * p + 1] = jnp.where(low, augk, yp).astype(BF16)

    ct, st = ct_ref[...], st_ref[...]
    nt = (((1,), (1,)), ((), ()))
    rowi = lax.broadcasted_iota(jnp.int32, (SUBLANES, tm), 0)
    zeros56 = jnp.zeros((QK_DIM - SUBLANES, tm), F32)

    yt = lax.dot_general(wt_ref[0:512, :], u, nt, preferred_element_type=F32)
    for h in range(N_DIFF_HEADS):
        tiles = [yt[h * LANES + r:h * LANES + r + SUBLANES] for r in range(0, LANES, SUBLANES)]
        for base in (0, QK_DIM // SUBLANES):
            x1, x2 = tiles[base], tiles[base + 1]
            tiles[base] = x1 * ct - x2 * st
            tiles[base + 1] = x2 * ct + x1 * st
        qt_ref[h] = (jnp.concatenate(tiles, axis=0) * scale).astype(BF16)

    yt = lax.dot_general(wt_ref[512:1024, :], u, nt, preferred_element_type=F32)
    for h in range(N_FOX_HEADS):
        p, e = divmod(h, 2)
        src = (QK_DIM if e == 0 else 0) + SUBLANES * p
        hi, mid, lo = _bf16_pieces(cst[src:src + SUBLANES])
        aug8 = jnp.where(rowi < 3, 1.0, jnp.where(rowi == 3, hi, jnp.where(
            rowi == 4, mid, jnp.where(rowi == 5, lo, 0.0))))
        qh = yt[h * QK_DIM:(h + 1) * QK_DIM] * scale
        blocks = [qh, aug8, zeros56] if e == 0 else [aug8, zeros56, qh]
        qt_ref[N_DIFF_HEADS + h] = jnp.concatenate(blocks, axis=0).astype(BF16)

    tk = vt_ref.shape[-1]
    for c in range(2):
        yt = lax.dot_general(wt_ref[1024 + 512 * c:1536 + 512 * c, :], u, nt,
                             preferred_element_type=F32)
        for j in range(4):
            blk = yt[j * LANES:(j + 1) * LANES].astype(BF16)
            for t in range(tm // tk):
                vt_ref[4 * c + j, t] = blk[:, t * tk:(t + 1) * tk]


def _inproj(x2, g, w_nat, w_tr, w_gate, b_gate, rope_nat, rope_tr, *, batch, seq, tm, tk):
    rows, d = x2.shape
    tiles_per_batch = seq // tm
    const = lambda i: (0, 0)
    tile = lambda i: (i % tiles_per_batch, 0)
    tile_t = lambda i: (0, i % tiles_per_batch)
    bt = lambda i: (i // tiles_per_batch, 0, i % tiles_per_batch, 0)
    bt5 = lambda i: (i // tiles_per_batch, 0, i % tiles_per_batch, 0, 0)
    return pl.pallas_call(
        functools.partial(_inproj_kernel, tiles_per_batch=tiles_per_batch, tm=tm),
        grid=(rows // tm,),
        in_specs=[
            pl.BlockSpec((tm, d), lambda i: (i, 0)),
            pl.BlockSpec((1, d), const),
            pl.BlockSpec(w_nat.shape, const),
            pl.BlockSpec(w_tr.shape, const),
            pl.BlockSpec(w_gate.shape, const),
            pl.BlockSpec((1, LANES), const),
            pl.BlockSpec((tm, LANES), tile),
            pl.BlockSpec((tm, LANES), tile),
            pl.BlockSpec((tm, LANES), tile),
            pl.BlockSpec((SUBLANES, tm), tile_t),
            pl.BlockSpec((SUBLANES, tm), tile_t),
        ],
        out_specs=[
            pl.BlockSpec((None, QT_BLOCKS, None, LANES, tm), bt5),
            pl.BlockSpec((None, K_BLOCKS, tm, LANES), bt),
            pl.BlockSpec((None, VT_BLOCKS, tm // tk, LANES, tk), bt5),
        ],
        out_shape=[
            jax.ShapeDtypeStruct((batch, QT_BLOCKS, seq // tm, LANES, tm), BF16),
            jax.ShapeDtypeStruct((batch, K_BLOCKS, seq, LANES), BF16),
            jax.ShapeDtypeStruct((batch, VT_BLOCKS, seq // tk, LANES, tk), BF16),
        ],
        scratch_shapes=[pltpu.VMEM((SUBLANES, LANES), F32)],
        compiler_params=pltpu.CompilerParams(
            dimension_semantics=("arbitrary",), vmem_limit_bytes=VMEM_LIMIT),
        name="inproj",
    )(x2, g, w_nat, w_tr, w_gate, b_gate, *rope_nat, *rope_tr)


def _attn_kernel(*refs, kind, tq, tk, seq):
    if kind == "diff":
        lam_ref, g_ref, qt_ref, k_ref, vt_ref, o_ref, acc_sc, qs_sc, s0_sc, s1_sc = refs
        k_refs = (k_ref,)
    else:
        g_ref, qt0_ref, qt1_ref, k0_ref, k1_ref, vt_ref, o_ref, acc_sc, qs_sc, s0_sc, s1_sc = refs
        k_refs = (k0_ref, k1_ref)
    n = 2 * tq
    assert tq == 2 * tk
    rowq = lax.broadcasted_iota(jnp.int32, (LANES, tq), 0)

    def scores(t, s_ref, masked):
        k0 = pl.multiple_of(t * tq, tq)
        w = n // len(k_refs)
        for j, kr in enumerate(k_refs):
            s = jnp.dot(kr[pl.ds(k0, tq), :], qs_sc[:, j * w:(j + 1) * w],
                        preferred_element_type=F32)
            if masked:
                key_l = lax.broadcasted_iota(jnp.int32, s.shape, 0)
                qry_l = lax.broadcasted_iota(jnp.int32, s.shape, 1) % tq
                s = jnp.where(key_l <= qry_l, s, NEG)
            s_ref[:, j * w:(j + 1) * w] = s

    def absorb(s_ref, t, ml):
        m, l = ml
        s = s_ref[...]
        m_next = jnp.maximum(m, jnp.max(s, axis=0, keepdims=True))
        alpha = jnp.exp2(m - m_next)
        p = jnp.exp2(s - m_next)
        l_next = alpha * l + jnp.sum(p, axis=0, keepdims=True)
        pb = p.astype(BF16)
        kb = 2 * t
        pv = (jnp.dot(vt_ref[kb], pb[:tk], preferred_element_type=F32)
              + jnp.dot(vt_ref[kb + 1], pb[tk:], preferred_element_type=F32))
        acc_sc[...] = acc_sc[...] * alpha + pv
        return m_next, l_next

    def q_tile(qi, half, odd):
        if kind == "diff":
            qt = qt_ref[qi]
            zero = jnp.zeros_like(qt)
            qs_sc[:, :tq] = jnp.where(rowq < QK_DIM, qt, zero)
            qs_sc[:, tq:] = jnp.where(rowq < QK_DIM, zero, qt)
        else:
            qs_sc[:, :tq] = qt0_ref[qi]
            qs_sc[:, tq:] = qt1_ref[qi]

        acc_sc[...] = jnp.zeros(acc_sc.shape, F32)
        ml = (jnp.full((1, n), NEG, F32), jnp.zeros((1, n), F32))
        key_step = lambda u: jnp.where(u == 0, qi, u - 1)
        scores(qi, s0_sc, True)

        def two_steps(w, ml):
            u = 2 * w
            scores(u, s1_sc, False)
            ml = absorb(s0_sc, key_step(u), ml)
            scores(u + 1, s0_sc, False)
            return absorb(s1_sc, u, ml)

        ml = lax.fori_loop(0, half, two_steps, ml)
        u = 2 * half
        if odd:
            scores(u, s1_sc, False)
            ml = absorb(s0_sc, key_step(u), ml)
            ml = absorb(s1_sc, u, ml)
        else:
            ml = absorb(s0_sc, key_step(u), ml)
        m, l = ml

        acc = acc_sc[...]
        o0 = acc[:, :tq] / l[:, :tq]
        o1 = acc[:, tq:] / l[:, tq:]
        g = g_ref[...]
        if kind == "diff":
            lam = (jnp.exp(jnp.sum(lam_ref[0:1, :] * lam_ref[1:2, :], axis=1, keepdims=True))
                   - jnp.exp(jnp.sum(lam_ref[2:3, :] * lam_ref[3:4, :], axis=1, keepdims=True))
                   + LAMBDA_INIT)
            o = o0 - lam * o1
            ms = jnp.mean(o * o, axis=0, keepdims=True)
            o = o * lax.rsqrt(ms + SUBLN_EPS) * g * (1.0 - LAMBDA_INIT)
        else:
            halves = []
            for oh in (o0[:QK_DIM], o1[QK_DIM:]):
                ms = jnp.mean(oh * oh, axis=0, keepdims=True)
                halves.append(oh * lax.rsqrt(ms + EPS))
            o = jnp.concatenate(halves, axis=0) * g
        o_ref[pl.ds(pl.multiple_of(qi * tq, tq), tq), :] = o.T.astype(o_ref.dtype)

    def tile_pair(half, carry):
        q_tile(2 * half, half, False)
        q_tile(2 * half + 1, half, True)
        return carry

    lax.fori_loop(0, seq // tq // 2, tile_pair, 0)


def _attention(kind, qt, kn, vt, extra, *, tq, tk):
    batch, _, nq, _, _ = qt.shape
    seq = kn.shape[2]
    nk = vt.shape[2]
    const = lambda b, h: (0, 0)
    qspec = lambda f: pl.BlockSpec((None, None, nq, LANES, tq), lambda b, h: (b, f(h), 0, 0, 0))
    kspec = lambda f: pl.BlockSpec((None, None, seq, LANES), lambda b, h: (b, f(h), 0, 0))
    if kind == "diff":
        lam4, g = extra
        in_specs = [pl.BlockSpec(lam4.shape, const), pl.BlockSpec(g.shape, const),
                    qspec(lambda h: h), kspec(lambda h: h)]
        args = [lam4, g, qt, kn]
        v_base = 0
    else:
        (g,) = extra
        in_specs = [pl.BlockSpec(g.shape, const),
                    qspec(lambda h: N_DIFF_HEADS + 2 * h), qspec(lambda h: N_DIFF_HEADS + 2 * h + 1),
                    kspec(lambda h: N_DIFF_HEADS + 2 * h), kspec(lambda h: N_DIFF_HEADS + 2 * h + 1)]
        args = [g, qt, qt, kn, kn]
        v_base = N_DIFF_HEADS
    in_specs.append(pl.BlockSpec((None, None, nk, LANES, tk), lambda b, h: (b, v_base + h, 0, 0, 0)))
    args.append(vt)
    n_groups = 4
    return pl.pallas_call(
        functools.partial(_attn_kernel, kind=kind, tq=tq, tk=tk, seq=seq),
        grid=(batch, n_groups),
        in_specs=in_specs,
        out_specs=pl.BlockSpec((None, seq, LANES), lambda b, h: (b, 0, h)),
        out_shape=jax.ShapeDtypeStruct((batch, seq, n_groups * LANES), BF16),
        scratch_shapes=[pltpu.VMEM((LANES, 2 * tq), F32), pltpu.VMEM((LANES, 2 * tq), BF16),
                        pltpu.VMEM((tq, 2 * tq), F32), pltpu.VMEM((tq, 2 * tq), F32)],
        compiler_params=pltpu.CompilerParams(
            dimension_semantics=("arbitrary", "arbitrary"), vmem_limit_bytes=VMEM_LIMIT),
        name=kind + "_attn",
    )(*args)


def _memkv_kernel(mem_ref, g_ref, w_ref, kv_ref):
    u = _rms(mem_ref[...], g_ref[...], EPS).astype(BF16)
    kv_ref[...] = jnp.dot(u, w_ref[...], preferred_element_type=F32).astype(BF16)


def _memkv(mem2, g, w, *, tm):
    rows, d = mem2.shape
    n = w.shape[1]
    return pl.pallas_call(
        _memkv_kernel,
        grid=(rows // tm,),
        in_specs=[pl.BlockSpec((tm, d), lambda i: (i, 0)),
                  pl.BlockSpec((1, d), lambda i: (0, 0)),
                  pl.BlockSpec(w.shape, lambda i: (0, 0))],
        out_specs=pl.BlockSpec((tm, n), lambda i: (i, 0)),
        out_shape=jax.ShapeDtypeStruct((rows, n), BF16),
        compiler_params=pltpu.CompilerParams(
            dimension_semantics=("arbitrary",), vmem_limit_bytes=VMEM_LIMIT),
        name="memkv",
    )(mem2, g, w)


def _cross_kernel(x_ref, d_ref, f_ref, wo_ref, g_ref, wq_ref, kv_ref, wco_ref, h_ref):
    half = d_ref.shape[1]
    h1 = (x_ref[...]
          + jnp.dot(d_ref[...], wo_ref[:half, :], preferred_element_type=F32)
          + jnp.dot(f_ref[...], wo_ref[half:, :], preferred_element_type=F32))
    uq = _rms(h1, g_ref[...], EPS).astype(BF16)
    dm = h1.shape[1]
    hd = dm // N_CROSS_HEADS
    cq = (jnp.dot(uq, wq_ref[...], preferred_element_type=F32) * (hd ** -0.5)).astype(BF16)
    outs = []
    for hh in range(N_CROSS_HEADS):
        qh = cq[:, hh * hd:(hh + 1) * hd]
        kh = kv_ref[:, hh * hd:(hh + 1) * hd]
        vh = kv_ref[:, dm + hh * hd:dm + (hh + 1) * hd]
        s = lax.dot_general(qh, kh, (((1,), (1,)), ((), ())), preferred_element_type=F32)
        p = jnp.exp(s - jnp.max(s, axis=1, keepdims=True))
        l = jnp.sum(p, axis=1, keepdims=True)
        oh = jnp.dot(p.astype(BF16), vh, preferred_element_type=F32) / l
        outs.append(oh.astype(BF16))
    co = jnp.concatenate(outs, axis=1)
    h_ref[...] = h1 + jnp.dot(co, wco_ref[...], preferred_element_type=F32)


def _cross(x2, d2, f2, w_out, g, w_cq, kv, w_co, *, seq, tm):
    rows, dm = x2.shape
    tiles_per_batch = seq // tm
    n_mem = kv.shape[1]
    const = lambda i: (0, 0)
    return pl.pallas_call(
        _cross_kernel,
        grid=(rows // tm,),
        in_specs=[
            pl.BlockSpec((tm, dm), lambda i: (i, 0)),
            pl.BlockSpec((tm, d2.shape[1]), lambda i: (i, 0)),
            pl.BlockSpec((tm, f2.shape[1]), lambda i: (i, 0)),
            pl.BlockSpec(w_out.shape, const),
            pl.BlockSpec((1, dm), const),
            pl.BlockSpec(w_cq.shape, const),
            pl.BlockSpec((None, n_mem, kv.shape[2]), lambda i: (i // tiles_per_batch, 0, 0)),
            pl.BlockSpec(w_co.shape, const),
        ],
        out_specs=pl.BlockSpec((tm, dm), lambda i: (i, 0)),
        out_shape=jax.ShapeDtypeStruct((rows, dm), F32),
        compiler_params=pltpu.CompilerParams(
            dimension_semantics=("arbitrary",), vmem_limit_bytes=VMEM_LIMIT),
        name="cross",
    )(x2, d2, f2, w_out, g, w_cq, kv, w_co)


def _mlp_kernel(h_ref, g_ref, wu_ref, wd_ref, gf_ref, o_ref, *, fc):
    h = h_ref[...]
    u = _rms(h, g_ref[...], EPS).astype(BF16)
    acc = h
    for c in range(wu_ref.shape[1] // fc):
        z = jnp.dot(u, wu_ref[:, c * fc:(c + 1) * fc], preferred_element_type=F32)
        a = jnp.square(jnp.maximum(z, 0.0)).astype(BF16)
        acc = acc + jnp.dot(a, wd_ref[c * fc:(c + 1) * fc, :], preferred_element_type=F32)
    o_ref[...] = _rms(acc, gf_ref[...], EPS)


def _mlp(h2, g, w_up, w_down, g_final, *, tm, fc):
    rows, dm = h2.shape
    const = lambda i: (0, 0)
    return pl.pallas_call(
        functools.partial(_mlp_kernel, fc=fc),
        grid=(rows // tm,),
        in_specs=[
            pl.BlockSpec((tm, dm), lambda i: (i, 0)),
            pl.BlockSpec((1, dm), const),
            pl.BlockSpec(w_up.shape, const, pipeline_mode=pl.Buffered(1)),
            pl.BlockSpec(w_down.shape, const, pipeline_mode=pl.Buffered(1)),
            pl.BlockSpec((1, dm), const),
        ],
        out_specs=pl.BlockSpec((tm, dm), lambda i: (i, 0)),
        out_shape=jax.ShapeDtypeStruct((rows, dm), F32),
        compiler_params=pltpu.CompilerParams(
            dimension_semantics=("arbitrary",), vmem_limit_bytes=VMEM_LIMIT),
        name="mlp",
    )(h2, g, w_up, w_down, g_final)


def _rope_tables(seq):
    pos = jnp.arange(seq, dtype=F32)
    inv_freq = ROPE_THETA ** (-jnp.arange(0, ROT_DIM, 2, dtype=F32) / ROT_DIM)
    ang = pos[:, None] * inv_freq[None, :]
    cos, sin = jnp.cos(ang), jnp.sin(ang)
    lane = jnp.arange(LANES)
    l64 = lane % QK_DIM
    fidx = l64 % (ROT_DIM // 2)
    in_rot = (l64 < ROT_DIM)[None, :]
    first = (l64 < ROT_DIM // 2)[None, :]
    rc = jnp.where(in_rot, cos[:, fidx], 1.0)
    ra = jnp.where(in_rot & ~first, sin[:, fidx], 0.0)
    rb = jnp.where(first, -sin[:, fidx], 0.0)
    return (rc.astype(F32), ra.astype(F32), rb.astype(F32)), (cos.T.astype(F32), sin.T.astype(F32))


def _gate_lane_layout(cols):
    odd = jnp.repeat(cols[..., 1::2], SUBLANES, axis=-1)
    even = jnp.repeat(cols[..., 0::2], SUBLANES, axis=-1)
    pad = jnp.zeros(cols.shape[:-1] + (QK_DIM - odd.shape[-1],), cols.dtype)
    return jnp.concatenate([odd, pad, even, pad], axis=-1)


def kernel(x, mem, norm_mix_g, w_in, b_forget, lam_q1, lam_k1, lam_q2, lam_k2,
           diff_subln_g, fox_out_g, w_out, norm_cross_g, norm_mem_g, w_cq, w_ckv, w_co,
           norm_mlp_g, w_up, w_down, norm_final_g):
    batch, seq, dm = x.shape
    n_mem = mem.shape[1]
    assert norm_mix_g.shape[0] == 1, "single-layer block"
    assert w_in.shape[2] == 6 * 512 + N_FOX_HEADS

    x2 = x.reshape(batch * seq, dm)
    w = w_in[0]
    dq, dk, dv, fq, fk, fv = (w[:, c * 512:(c + 1) * 512] for c in range(6))
    w_nat = jnp.concatenate([dk, fk], axis=1).astype(BF16)
    w_tr = jnp.concatenate([dq, fq, dv, fv], axis=1).T.astype(BF16)
    w_gate = _gate_lane_layout(w[:, 6 * 512:]).astype(BF16)
    b_gate = _gate_lane_layout(b_forget[0]).reshape(1, LANES)
    rope_nat, rope_tr = _rope_tables(seq)

    qt, kn, vt = _inproj(x2, norm_mix_g[0].reshape(1, dm), w_nat, w_tr, w_gate, b_gate,
                         rope_nat, rope_tr, batch=batch, seq=seq, tm=ROW_TILE, tk=ATTN_TK)

    lam4 = jnp.stack([lam_q1[0], lam_k1[0], lam_q2[0], lam_k2[0]], axis=0)
    d_out = _attention("diff", qt, kn, vt, (lam4, diff_subln_g[0].reshape(LANES, 1)),
                       tq=ATTN_TQ, tk=ATTN_TK)
    f_out = _attention("fox", qt, kn, vt, (jnp.tile(fox_out_g[0], 2).reshape(LANES, 1),),
                       tq=ATTN_TQ, tk=ATTN_TK)

    kv = _memkv(mem.reshape(batch * n_mem, dm), norm_mem_g[0].reshape(1, dm),
                w_ckv[0].astype(BF16), tm=ROW_TILE)
    h2 = _cross(x2, d_out.reshape(batch * seq, -1), f_out.reshape(batch * seq, -1),
                w_out[0].astype(BF16), norm_cross_g[0].reshape(1, dm), w_cq[0].astype(BF16),
                kv.reshape(batch, n_mem, -1), w_co[0].astype(BF16), seq=seq, tm=ROW_TILE)
    out = _mlp(h2, norm_mlp_g[0].reshape(1, dm), w_up[0].astype(BF16), w_down[0].astype(BF16),
               norm_final_g.reshape(1, dm), tm=ROW_TILE, fc=1024)
    return out.reshape(batch, seq, dm)
```

```python
import functools

import jax
import jax.numpy as jnp
from jax import lax
from jax.experimental import pallas as pl
from jax.experimental.pallas import tpu as pltpu

F32 = jnp.float32
BF16 = jnp.bfloat16

LANES = 128
SUBLANES = 8
MXU_COLS = 256
QK_DIM = 64
ROT_DIM = 16
ROPE_THETA = 500000.0
N_DIFF_HEADS = 4
N_FOX_HEADS = 8
N_FOX_PAIRS = N_FOX_HEADS // 2
N_CROSS_HEADS = 4
EPS = 1e-6
SUBLN_EPS = 1e-5
LAMBDA_INIT = 0.8 - 0.6 * 1.0
NEG = -1e30
LOG2E = 1.4426950408889634

VMEM_LIMIT = 56 * 1024 * 1024

ROW_TILE = 512
ATTN_TQ = 512
ATTN_TK = 512

QT_BLOCKS = N_DIFF_HEADS + N_FOX_HEADS
K_BLOCKS = N_DIFF_HEADS + N_FOX_HEADS
VT_BLOCKS = N_DIFF_HEADS + N_FOX_PAIRS
ONES_ROWS = 16
VT_ROWS = LANES + ONES_ROWS


def _rms(x, g, eps):
    ms = jnp.mean(x * x, axis=-1, keepdims=True)
    return x * lax.rsqrt(ms + eps) * g


def _bf16_pieces(c):
    hi = c.astype(BF16).astype(F32)
    r1 = c - hi
    mid = r1.astype(BF16).astype(F32)
    lo = (r1 - mid).astype(BF16).astype(F32)
    return hi, mid, lo


def _inproj_kernel(x_ref, g_ref, wn_ref, wt_ref, wg_ref, bg_ref, rc_ref, ra_ref, rb_ref,
                   ct_ref, st_ref, qt_ref, k_ref, vt_ref, carry_ref, *, tiles_per_batch, tm):
    i = pl.program_id(0)
    u = _rms(x_ref[...], g_ref[...], EPS).astype(BF16)
    scale = QK_DIM ** -0.5 * LOG2E

    gate = jnp.dot(u, wg_ref[...], preferred_element_type=F32) + bg_ref[...]
    logf = jnp.minimum(gate, 0.0) - jnp.log1p(jnp.exp(-jnp.abs(gate)))

    @pl.when(i % tiles_per_batch == 0)
    def _():
        carry_ref[...] = jnp.zeros_like(carry_ref)

    sub = 256
    row = lax.broadcasted_iota(jnp.int32, (sub, sub), 0)
    col = lax.broadcasted_iota(jnp.int32, (sub, sub), 1)
    tri = (row >= col).astype(BF16)
    carry = carry_ref[0:1, :]
    parts = []
    for r in range(tm // sub):
        hi, mid, lo = _bf16_pieces(logf[r * sub:(r + 1) * sub])
        cs = (jnp.dot(tri, hi.astype(BF16), preferred_element_type=F32)
              + jnp.dot(tri, mid.astype(BF16), preferred_element_type=F32)
              + jnp.dot(tri, lo.astype(BF16), preferred_element_type=F32)) + carry
        carry = cs[sub - 1:sub, :]
        parts.append(cs)
    carry_ref[...] = jnp.broadcast_to(carry, carry_ref.shape)
    cs = jnp.concatenate(parts, axis=0) * LOG2E
    cst = cs.T

    rc, ra, rb = rc_ref[...], ra_ref[...], rb_ref[...]
    lane = lax.broadcasted_iota(jnp.int32, (tm, LANES), 1)
    low = lane < QK_DIM
    l64 = lane % QK_DIM
    yn = jnp.dot(u, wn_ref[...], preferred_element_type=F32)
    for h in range(N_DIFF_HEADS):
        t = yn[:, h * LANES:(h + 1) * LANES]
        t = t * rc + pltpu.roll(t, 8, 1) * ra + pltpu.roll(t, LANES - 8, 1) * rb
        k_ref[h] = t.astype(BF16)
    for p in range(N_FOX_PAIRS):
        yp = yn[:, (N_DIFF_HEADS + p) * LANES:(N_DIFF_HEADS + p + 1) * LANES]
        rolled = cs if p == 0 else pltpu.roll(cs, LANES - 8 * p, 1)
        hi, mid, lo = _bf16_pieces(rolled)
        augk = jnp.where(l64 == 0, -hi, jnp.where(l64 == 1, -mid, jnp.where(
            l64 == 2, -lo, jnp.where(l64 < 6, 1.0, 0.0))))
        k_ref[N_DIFF_HEADS + 2 * p] = jnp.where(low, yp, augk).astype(BF16)
        k_ref[N_DIFF_HEADS + 2 * p + 1] = jnp.where(low, augk, yp).astype(BF16)

    ct, st = ct_ref[...], st_ref[...]
    nt = (((1,), (1,)), ((), ()))
    rowi = lax.broadcasted_iota(jnp.int32, (SUBLANES, tm), 0)
    zeros56 = jnp.zeros((QK_DIM - SUBLANES, tm), F32)

    yt = lax.dot_general(wt_ref[0:512, :], u, nt, preferred_element_type=F32)
    for h in range(N_DIFF_HEADS):
        tiles = [yt[h * LANES + r:h * LANES + r + SUBLANES] for r in range(0, LANES, SUBLANES)]
        for base in (0, QK_DIM // SUBLANES):
            x1, x2 = tiles[base], tiles[base + 1]
            tiles[base] = x1 * ct - x2 * st
            tiles[base + 1] = x2 * ct + x1 * st
        qt_ref[h] = (jnp.concatenate(tiles, axis=0) * scale).astype(BF16)

    yt = lax.dot_general(wt_ref[512:1024, :], u, nt, preferred_element_type=F32)
    for h in range(N_FOX_HEADS):
        p, e = divmod(h, 2)
        src = (QK_DIM if e == 0 else 0) + SUBLANES * p
        hi, mid, lo = _bf16_pieces(cst[src:src + SUBLANES])
        aug8 = jnp.where(rowi < 3, 1.0, jnp.where(rowi == 3, hi, jnp.where(
            rowi == 4, mid, jnp.where(rowi == 5, lo, 0.0))))
        qh = yt[h * QK_DIM:(h + 1) * QK_DIM] * scale
        blocks = [qh, aug8, zeros56] if e == 0 else [aug8, zeros56, qh]
        qt_ref[N_DIFF_HEADS + h] = jnp.concatenate(blocks, axis=0).astype(BF16)

    tk = vt_ref.shape[-1]
    for c in range(2):
        yt = lax.dot_general(wt_ref[1024 + 512 * c:1536 + 512 * c, :], u, nt,
                             preferred_element_type=F32)
        for j in range(4):
            blk = jnp.concatenate([yt[j * LANES:(j + 1) * LANES], jnp.ones((ONES_ROWS, tm), F32)],
                                  axis=0).astype(BF16)
            for t in range(tm // tk):
                vt_ref[4 * c + j, t] = blk[:, t * tk:(t + 1) * tk]


def _inproj(x2, g, w_nat, w_tr, w_gate, b_gate, rope_nat, rope_tr, *, batch, seq, tm, tk):
    rows, d = x2.shape
    tiles_per_batch = seq // tm
    const = lambda i: (0, 0)
    tile = lambda i: (i % tiles_per_batch, 0)
    tile_t = lambda i: (0, i % tiles_per_batch)
    bt = lambda i: (i // tiles_per_batch, 0, i % tiles_per_batch, 0)
    bt5 = lambda i: (i // tiles_per_batch, 0, i % tiles_per_batch, 0, 0)
    return pl.pallas_call(
        functools.partial(_inproj_kernel, tiles_per_batch=tiles_per_batch, tm=tm),
        grid=(rows // tm,),
        in_specs=[
            pl.BlockSpec((tm, d), lambda i: (i, 0)),
            pl.BlockSpec((1, d), const),
            pl.BlockSpec(w_nat.shape, const),
            pl.BlockSpec(w_tr.shape, const),
            pl.BlockSpec(w_gate.shape, const),
            pl.BlockSpec((1, LANES), const),
            pl.BlockSpec((tm, LANES), tile),
            pl.BlockSpec((tm, LANES), tile),
            pl.BlockSpec((tm, LANES), tile),
            pl.BlockSpec((SUBLANES, tm), tile_t),
            pl.BlockSpec((SUBLANES, tm), tile_t),
        ],
        out_specs=[
            pl.BlockSpec((None, QT_BLOCKS, None, LANES, tm), bt5),
            pl.BlockSpec((None, K_BLOCKS, tm, LANES), bt),
            pl.BlockSpec((None, VT_BLOCKS, tm // tk, VT_ROWS, tk), bt5),
        ],
        out_shape=[
            jax.ShapeDtypeStruct((batch, QT_BLOCKS, seq // tm, LANES, tm), BF16),
            jax.ShapeDtypeStruct((batch, K_BLOCKS, seq, LANES), BF16),
            jax.ShapeDtypeStruct((batch, VT_BLOCKS, seq // tk, VT_ROWS, tk), BF16),
        ],
        scratch_shapes=[pltpu.VMEM((SUBLANES, LANES), F32)],
        compiler_params=pltpu.CompilerParams(
            dimension_semantics=("arbitrary",), vmem_limit_bytes=VMEM_LIMIT),
        name="inproj",
    )(x2, g, w_nat, w_tr, w_gate, b_gate, *rope_nat, *rope_tr)


def _attn_kernel(*refs, kind, tq, tk, seq):
    if kind == "diff":
        lam_ref, g_ref, qt_ref, k_ref, vt_ref, o_ref, acc_sc, qs_sc, s0_sc, s1_sc = refs
        k_refs = (k_ref,)
    else:
        g_ref, qt0_ref, qt1_ref, k0_ref, k1_ref, vt_ref, o_ref, acc_sc, qs_sc, s0_sc, s1_sc = refs
        k_refs = (k0_ref, k1_ref)
    n = 2 * tq
    assert tq == tk
    rowq = lax.broadcasted_iota(jnp.int32, (LANES, tq), 0)

    def scores(t, s_ref, masked):
        k0 = pl.multiple_of(t * tk, tk)
        w = n // len(k_refs)
        maxes = []
        for j, kr in enumerate(k_refs):
            s = jnp.dot(kr[pl.ds(k0, tk), :], qs_sc[:, j * w:(j + 1) * w],
                        preferred_element_type=F32)
            if masked:
                key_l = lax.broadcasted_iota(jnp.int32, s.shape, 0)
                qry_l = lax.broadcasted_iota(jnp.int32, s.shape, 1) % tq
                s = jnp.where(key_l <= qry_l, s, NEG)
            s_ref[:, j * w:(j + 1) * w] = s
            maxes.append(jnp.max(s, axis=0, keepdims=True))
        return maxes[0] if len(maxes) == 1 else jnp.concatenate(maxes, axis=1)

    def absorb(s_ref, m_cur, t, ml):
        m, l = ml
        m_next = jnp.maximum(m, m_cur)
        alpha = jnp.exp2(m - m_next)
        p = jnp.exp2(s_ref[...] - m_next).astype(BF16)
        pv = jnp.dot(vt_ref[t], p, preferred_element_type=F32)
        acc_sc[...] = acc_sc[...] * alpha + pv[:LANES]
        return m_next, alpha * l + pv[LANES:LANES + 1]

    def q_tile(qi, half, odd):
        if kind == "diff":
            qt = qt_ref[qi]
            zero = jnp.zeros_like(qt)
            qs_sc[:, :tq] = jnp.where(rowq < QK_DIM, qt, zero)
            qs_sc[:, tq:] = jnp.where(rowq < QK_DIM, zero, qt)
        else:
            qs_sc[:, :tq] = qt0_ref[qi]
            qs_sc[:, tq:] = qt1_ref[qi]

        acc_sc[...] = jnp.zeros(acc_sc.shape, F32)
        ml = (jnp.full((1, n), NEG, F32), jnp.zeros((1, n), F32))
        key_step = lambda u: jnp.where(u == 0, qi, u - 1)
        mc0 = scores(qi, s0_sc, True)

        def two_steps(w, carry):
            ml, mc0 = carry
            u = 2 * w
            mc1 = scores(u, s1_sc, False)
            ml = absorb(s0_sc, mc0, key_step(u), ml)
            mc0 = scores(u + 1, s0_sc, False)
            return absorb(s1_sc, mc1, u, ml), mc0

        ml, mc0 = lax.fori_loop(0, half, two_steps, (ml, mc0))
        u = 2 * half
        if odd:
            mc1 = scores(u, s1_sc, False)
            ml = absorb(s0_sc, mc0, key_step(u), ml)
            ml = absorb(s1_sc, mc1, u, ml)
        else:
            ml = absorb(s0_sc, mc0, key_step(u), ml)
        m, l = ml

        acc = acc_sc[...]
        o0 = acc[:, :tq] / l[:, :tq]
        o1 = acc[:, tq:] / l[:, tq:]
        g = g_ref[...]
        if kind == "diff":
            lam = (jnp.exp(jnp.sum(lam_ref[0:1, :] * lam_ref[1:2, :], axis=1, keepdims=True))
                   - jnp.exp(jnp.sum(lam_ref[2:3, :] * lam_ref[3:4, :], axis=1, keepdims=True))
                   + LAMBDA_INIT)
            o = o0 - lam * o1
            ms = jnp.mean(o * o, axis=0, keepdims=True)
            o = o * lax.rsqrt(ms + SUBLN_EPS) * g * (1.0 - LAMBDA_INIT)
        else:
            halves = []
            for oh in (o0[:QK_DIM], o1[QK_DIM:]):
                ms = jnp.mean(oh * oh, axis=0, keepdims=True)
                halves.append(oh * lax.rsqrt(ms + EPS))
            o = jnp.concatenate(halves, axis=0) * g
        o_ref[pl.ds(pl.multiple_of(qi * tq, tq), tq), :] = o.T.astype(o_ref.dtype)

    def tile_pair(half, carry):
        q_tile(2 * half, half, False)
        q_tile(2 * half + 1, half, True)
        return carry

    lax.fori_loop(0, seq // tq // 2, tile_pair, 0)


def _attention(kind, qt, kn, vt, extra, *, tq, tk):
    batch, _, nq, _, _ = qt.shape
    seq = kn.shape[2]
    nk = vt.shape[2]
    const = lambda b, h: (0, 0)
    qspec = lambda f: pl.BlockSpec((None, None, nq, LANES, tq), lambda b, h: (b, f(h), 0, 0, 0))
    kspec = lambda f: pl.BlockSpec((None, None, seq, LANES), lambda b, h: (b, f(h), 0, 0))
    if kind == "diff":
        lam4, g = extra
        in_specs = [pl.BlockSpec(lam4.shape, const), pl.BlockSpec(g.shape, const),
                    qspec(lambda h: h), kspec(lambda h: h)]
        args = [lam4, g, qt, kn]
        v_base = 0
    else:
        (g,) = extra
        in_specs = [pl.BlockSpec(g.shape, const),
                    qspec(lambda h: N_DIFF_HEADS + 2 * h), qspec(lambda h: N_DIFF_HEADS + 2 * h + 1),
                    kspec(lambda h: N_DIFF_HEADS + 2 * h), kspec(lambda h: N_DIFF_HEADS + 2 * h + 1)]
        args = [g, qt, qt, kn, kn]
        v_base = N_DIFF_HEADS
    in_specs.append(pl.BlockSpec((None, None, nk, VT_ROWS, tk), lambda b, h: (b, v_base + h, 0, 0, 0)))
    args.append(vt)
    n_groups = 4
    return pl.pallas_call(
        functools.partial(_attn_kernel, kind=kind, tq=tq, tk=tk, seq=seq),
        grid=(batch, n_groups),
        in_specs=in_specs,
        out_specs=pl.BlockSpec((None, seq, LANES), lambda b, h: (b, 0, h)),
        out_shape=jax.ShapeDtypeStruct((batch, seq, n_groups * LANES), BF16),
        scratch_shapes=[pltpu.VMEM((LANES, 2 * tq), F32), pltpu.VMEM((LANES, 2 * tq), BF16),
                        pltpu.VMEM((tq, 2 * tq), F32), pltpu.VMEM((tq, 2 * tq), F32)],
        compiler_params=pltpu.CompilerParams(
            dimension_semantics=("arbitrary", "arbitrary"), vmem_limit_bytes=VMEM_LIMIT),
        name=kind + "_attn",
    )(*args)


def _memkv_kernel(mem_ref, g_ref, w_ref, kv_ref):
    u = _rms(mem_ref[...], g_ref[...], EPS).astype(BF16)
    kv_ref[...] = jnp.dot(u, w_ref[...], preferred_element_type=F32).astype(BF16)


def _memkv(mem2, g, w, *, tm):
    rows, d = mem2.shape
    n = w.shape[1]
    return pl.pallas_call(
        _memkv_kernel,
        grid=(rows // tm,),
        in_specs=[pl.BlockSpec((tm, d), lambda i: (i, 0)),
                  pl.BlockSpec((1, d), lambda i: (0, 0)),
                  pl.BlockSpec(w.shape, lambda i: (0, 0))],
        out_specs=pl.BlockSpec((tm, n), lambda i: (i, 0)),
        out_shape=jax.ShapeDtypeStruct((rows, n), BF16),
        compiler_params=pltpu.CompilerParams(
            dimension_semantics=("arbitrary",), vmem_limit_bytes=VMEM_LIMIT),
        name="memkv",
    )(mem2, g, w)


def _cross_kernel(x_ref, d_ref, f_ref, wo_ref, g_ref, wq_ref, kv_ref, wco_ref, h_ref):
    half = d_ref.shape[1]
    h1 = (x_ref[...]
          + jnp.dot(d_ref[...], wo_ref[:half, :], preferred_element_type=F32)
          + jnp.dot(f_ref[...], wo_ref[half:, :], preferred_element_type=F32))
    uq = _rms(h1, g_ref[...], EPS).astype(BF16)
    dm = h1.shape[1]
    hd = dm // N_CROSS_HEADS
    cq = (jnp.dot(uq, wq_ref[...], preferred_element_type=F32) * (hd ** -0.5)).astype(BF16)
    outs = []
    for hh in range(N_CROSS_HEADS):
        qh = cq[:, hh * hd:(hh + 1) * hd]
        kh = kv_ref[:, hh * hd:(hh + 1) * hd]
        vh = kv_ref[:, dm + hh * hd:dm + (hh + 1) * hd]
        s = lax.dot_general(qh, kh, (((1,), (1,)), ((), ())), preferred_element_type=F32)
        p = jnp.exp(s - jnp.max(s, axis=1, keepdims=True))
        l = jnp.sum(p, axis=1, keepdims=True)
        oh = jnp.dot(p.astype(BF16), vh, preferred_element_type=F32) / l
        outs.append(oh.astype(BF16))
    co = jnp.concatenate(outs, axis=1)
    h_ref[...] = h1 + jnp.dot(co, wco_ref[...], preferred_element_type=F32)


def _cross(x2, d2, f2, w_out, g, w_cq, kv, w_co, *, seq, tm):
    rows, dm = x2.shape
    tiles_per_batch = seq // tm
    n_mem = kv.shape[1]
    const = lambda i: (0, 0)
    return pl.pallas_call(
        _cross_kernel,
        grid=(rows // tm,),
        in_specs=[
            pl.BlockSpec((tm, dm), lambda i: (i, 0)),
            pl.BlockSpec((tm, d2.shape[1]), lambda i: (i, 0)),
            pl.BlockSpec((tm, f2.shape[1]), lambda i: (i, 0)),
            pl.BlockSpec(w_out.shape, const),
            pl.BlockSpec((1, dm), const),
            pl.BlockSpec(w_cq.shape, const),
            pl.BlockSpec((None, n_mem, kv.shape[2]), lambda i: (i // tiles_per_batch, 0, 0)),
            pl.BlockSpec(w_co.shape, const),
        ],
        out_specs=pl.BlockSpec((tm, dm), lambda i: (i, 0)),
        out_shape=jax.ShapeDtypeStruct((rows, dm), F32),
        compiler_params=pltpu.CompilerParams(
            dimension_semantics=("arbitrary",), vmem_limit_bytes=VMEM_LIMIT),
        name="cross",
    )(x2, d2, f2, w_out, g, w_cq, kv, w_co)


def _mlp_kernel(h_ref, g_ref, wu_ref, wd_ref, gf_ref, o_ref, *, fc):
    h = h_ref[...]
    u = _rms(h, g_ref[...], EPS).astype(BF16)
    acc = h
    for c in range(wu_ref.shape[1] // fc):
        z = jnp.dot(u, wu_ref[:, c * fc:(c + 1) * fc], preferred_element_type=F32)
        a = jnp.square(jnp.maximum(z, 0.0)).astype(BF16)
        acc = acc + jnp.dot(a, wd_ref[c * fc:(c + 1) * fc, :], preferred_element_type=F32)
    o_ref[...] = _rms(acc, gf_ref[...], EPS)


def _mlp(h2, g, w_up, w_down, g_final, *, tm, fc):
    rows, dm = h2.shape
    const = lambda i: (0, 0)
    return pl.pallas_call(
        functools.partial(_mlp_kernel, fc=fc),
        grid=(rows // tm,),
        in_specs=[
            pl.BlockSpec((tm, dm), lambda i: (i, 0)),
            pl.BlockSpec((1, dm), const),
            pl.BlockSpec(w_up.shape, const, pipeline_mode=pl.Buffered(1)),
            pl.BlockSpec(w_down.shape, const, pipeline_mode=pl.Buffered(1)),
            pl.BlockSpec((1, dm), const),
        ],
        out_specs=pl.BlockSpec((tm, dm), lambda i: (i, 0)),
        out_shape=jax.ShapeDtypeStruct((rows, dm), F32),
        compiler_params=pltpu.CompilerParams(
            dimension_semantics=("arbitrary",), vmem_limit_bytes=VMEM_LIMIT),
        name="mlp",
    )(h2, g, w_up, w_down, g_final)


def _rope_tables(seq):
    pos = jnp.arange(seq, dtype=F32)
    inv_freq = ROPE_THETA ** (-jnp.arange(0, ROT_DIM, 2, dtype=F32) / ROT_DIM)
    ang = pos[:, None] * inv_freq[None, :]
    cos, sin = jnp.cos(ang), jnp.sin(ang)
    lane = jnp.arange(LANES)
    l64 = lane % QK_DIM
    fidx = l64 % (ROT_DIM // 2)
    in_rot = (l64 < ROT_DIM)[None, :]
    first = (l64 < ROT_DIM // 2)[None, :]
    rc = jnp.where(in_rot, cos[:, fidx], 1.0)
    ra = jnp.where(in_rot & ~first, sin[:, fidx], 0.0)
    rb = jnp.where(first, -sin[:, fidx], 0.0)
    return (rc.astype(F32), ra.astype(F32), rb.astype(F32)), (cos.T.astype(F32), sin.T.astype(F32))


def _gate_lane_layout(cols):
    odd = jnp.repeat(cols[..., 1::2], SUBLANES, axis=-1)
    even = jnp.repeat(cols[..., 0::2], SUBLANES, axis=-1)
    pad = jnp.zeros(cols.shape[:-1] + (QK_DIM - odd.shape[-1],), cols.dtype)
    return jnp.concatenate([odd, pad, even, pad], axis=-1)


def kernel(x, mem, norm_mix_g, w_in, b_forget, lam_q1, lam_k1, lam_q2, lam_k2,
           diff_subln_g, fox_out_g, w_out, norm_cross_g, norm_mem_g, w_cq, w_ckv, w_co,
           norm_mlp_g, w_up, w_down, norm_final_g):
    batch, seq, dm = x.shape
    n_mem = mem.shape[1]
    assert norm_mix_g.shape[0] == 1, "single-layer block"
    assert w_in.shape[2] == 6 * 512 + N_FOX_HEADS

    x2 = x.reshape(batch * seq, dm)
    w = w_in[0]
    dq, dk, dv, fq, fk, fv = (w[:, c * 512:(c + 1) * 512] for c in range(6))
    w_nat = jnp.concatenate([dk, fk], axis=1).astype(BF16)
    w_tr = jnp.concatenate([dq, fq, dv, fv], axis=1).T.astype(BF16)
    w_gate = _gate_lane_layout(w[:, 6 * 512:]).astype(BF16)
    b_gate = _gate_lane_layout(b_forget[0]).reshape(1, LANES)
    rope_nat, rope_tr = _rope_tables(seq)

    qt, kn, vt = _inproj(x2, norm_mix_g[0].reshape(1, dm), w_nat, w_tr, w_gate, b_gate,
                         rope_nat, rope_tr, batch=batch, seq=seq, tm=ROW_TILE, tk=ATTN_TK)

    lam4 = jnp.stack([lam_q1[0], lam_k1[0], lam_q2[0], lam_k2[0]], axis=0)
    d_out = _attention("diff", qt, kn, vt, (lam4, diff_subln_g[0].reshape(LANES, 1)),
                       tq=ATTN_TQ, tk=ATTN_TK)
    f_out = _attention("fox", qt, kn, vt, (jnp.tile(fox_out_g[0], 2).reshape(LANES, 1),),
                       tq=ATTN_TQ, tk=ATTN_TK)

    kv = _memkv(mem.reshape(batch * n_mem, dm), norm_mem_g[0].reshape(1, dm),
                w_ckv[0].astype(BF16), tm=ROW_TILE)
    h2 = _cross(x2, d_out.reshape(batch * seq, -1), f_out.reshape(batch * seq, -1),
                w_out[0].astype(BF16), norm_cross_g[0].reshape(1, dm), w_cq[0].astype(BF16),
                kv.reshape(batch, n_mem, -1), w_co[0].astype(BF16), seq=seq, tm=ROW_TILE)
    out = _mlp(h2, norm_mlp_g[0].reshape(1, dm), w_up[0].astype(BF16), w_down[0].astype(BF16),
               norm_final_g.reshape(1, dm), tm=ROW_TILE, fc=1024)
    return out.reshape(batch, seq, dm)
```

```python
import functools

import jax
import jax.numpy as jnp
from jax import lax
from jax.experimental import pallas as pl
from jax.experimental.pallas import tpu as pltpu

F32 = jnp.float32
BF16 = jnp.bfloat16

LANES = 128
SUBLANES = 8
MXU_COLS = 256
QK_DIM = 64
ROT_DIM = 16
ROPE_THETA = 500000.0
N_DIFF_HEADS = 4
N_FOX_HEADS = 8
N_FOX_PAIRS = N_FOX_HEADS // 2
N_CROSS_HEADS = 4
EPS = 1e-6
SUBLN_EPS = 1e-5
LAMBDA_INIT = 0.8 - 0.6 * 1.0
NEG = -1e30
LOG2E = 1.4426950408889634

VMEM_LIMIT = 56 * 1024 * 1024

ROW_TILE = 512
ATTN_TQ = 512
ATTN_TK = 512
MLP_FF_CHUNK = 1024

QT_BLOCKS = N_DIFF_HEADS + N_FOX_HEADS
K_BLOCKS = N_DIFF_HEADS + N_FOX_HEADS
VT_BLOCKS = N_DIFF_HEADS + N_FOX_PAIRS
ONES_ROWS = 16
VT_ROWS = LANES + ONES_ROWS


def _rms(x, g, eps):
    ms = jnp.mean(x * x, axis=-1, keepdims=True)
    return x * lax.rsqrt(ms + eps) * g


def _bf16_pieces(c):
    hi = c.astype(BF16).astype(F32)
    r1 = c - hi
    mid = r1.astype(BF16).astype(F32)
    lo = (r1 - mid).astype(BF16).astype(F32)
    return hi, mid, lo


def _inproj_kernel(x_ref, g_ref, wn_ref, wt_ref, wg_ref, bg_ref, rc_ref, ra_ref, rb_ref,
                   ct_ref, st_ref, qt_ref, k_ref, vt_ref, carry_ref, *, tiles_per_batch, tm):
    i = pl.program_id(0)
    u = _rms(x_ref[...], g_ref[...], EPS).astype(BF16)
    scale = QK_DIM ** -0.5 * LOG2E

    gate = jnp.dot(u, wg_ref[...], preferred_element_type=F32) + bg_ref[...]
    logf = jnp.minimum(gate, 0.0) - jnp.log1p(jnp.exp(-jnp.abs(gate)))

    @pl.when(i % tiles_per_batch == 0)
    def _():
        carry_ref[...] = jnp.zeros_like(carry_ref)

    sub = 256
    row = lax.broadcasted_iota(jnp.int32, (sub, sub), 0)
    col = lax.broadcasted_iota(jnp.int32, (sub, sub), 1)
    tri = (row >= col).astype(BF16)
    carry = carry_ref[0:1, :]
    parts = []
    for r in range(tm // sub):
        hi, mid, lo = _bf16_pieces(logf[r * sub:(r + 1) * sub])
        cs = (jnp.dot(tri, hi.astype(BF16), preferred_element_type=F32)
              + jnp.dot(tri, mid.astype(BF16), preferred_element_type=F32)
              + jnp.dot(tri, lo.astype(BF16), preferred_element_type=F32)) + carry
        carry = cs[sub - 1:sub, :]
        parts.append(cs)
    carry_ref[...] = jnp.broadcast_to(carry, carry_ref.shape)
    cs = jnp.concatenate(parts, axis=0) * LOG2E
    cst = cs.T

    rc, ra, rb = rc_ref[...], ra_ref[...], rb_ref[...]
    lane = lax.broadcasted_iota(jnp.int32, (tm, LANES), 1)
    low = lane < QK_DIM
    l64 = lane % QK_DIM
    yn = jnp.dot(u, wn_ref[...], preferred_element_type=F32)
    for h in range(N_DIFF_HEADS):
        t = yn[:, h * LANES:(h + 1) * LANES]
        t = t * rc + pltpu.roll(t, 8, 1) * ra + pltpu.roll(t, LANES - 8, 1) * rb
        k_ref[h] = t.astype(BF16)
    for p in range(N_FOX_PAIRS):
        yp = yn[:, (N_DIFF_HEADS + p) * LANES:(N_DIFF_HEADS + p + 1) * LANES]
        rolled = cs if p == 0 else pltpu.roll(cs, LANES - 8 * p, 1)
        hi, mid, lo = _bf16_pieces(rolled)
        augk = jnp.where(l64 == 0, -hi, jnp.where(l64 == 1, -mid, jnp.where(
            l64 == 2, -lo, jnp.where(l64 < 6, 1.0, 0.0))))
        k_ref[N_DIFF_HEADS + 2 * p] = jnp.where(low, yp, augk).astype(BF16)
        k_ref[N_DIFF_HEADS + 2 * p + 1] = jnp.where(low, augk, yp).astype(BF16)

    ct, st = ct_ref[...], st_ref[...]
    nt = (((1,), (1,)), ((), ()))
    rowi = lax.broadcasted_iota(jnp.int32, (SUBLANES, tm), 0)
    zeros56 = jnp.zeros((QK_DIM - SUBLANES, tm), F32)

    yt = lax.dot_general(wt_ref[0:512, :], u, nt, preferred_element_type=F32)
    for h in range(N_DIFF_HEADS):
        tiles = [yt[h * LANES + r:h * LANES + r + SUBLANES] for r in range(0, LANES, SUBLANES)]
        for base in (0, QK_DIM // SUBLANES):
            x1, x2 = tiles[base], tiles[base + 1]
            tiles[base] = x1 * ct - x2 * st
            tiles[base + 1] = x2 * ct + x1 * st
        qt_ref[h] = (jnp.concatenate(tiles, axis=0) * scale).astype(BF16)

    yt = lax.dot_general(wt_ref[512:1024, :], u, nt, preferred_element_type=F32)
    for h in range(N_FOX_HEADS):
        p, e = divmod(h, 2)
        src = (QK_DIM if e == 0 else 0) + SUBLANES * p
        hi, mid, lo = _bf16_pieces(cst[src:src + SUBLANES])
        aug8 = jnp.where(rowi < 3, 1.0, jnp.where(rowi == 3, hi, jnp.where(
            rowi == 4, mid, jnp.where(rowi == 5, lo, 0.0))))
        qh = yt[h * QK_DIM:(h + 1) * QK_DIM] * scale
        blocks = [qh, aug8, zeros56] if e == 0 else [aug8, zeros56, qh]
        qt_ref[N_DIFF_HEADS + h] = jnp.concatenate(blocks, axis=0).astype(BF16)

    tk = vt_ref.shape[-1]
    for c in range(2):
        yt = lax.dot_general(wt_ref[1024 + 512 * c:1536 + 512 * c, :], u, nt,
                             preferred_element_type=F32)
        for j in range(4):
            blk = jnp.concatenate([yt[j * LANES:(j + 1) * LANES], jnp.ones((ONES_ROWS, tm), F32)],
                                  axis=0).astype(BF16)
            for t in range(tm // tk):
                vt_ref[4 * c + j, t] = blk[:, t * tk:(t + 1) * tk]


def _inproj(x2, g, w_nat, w_tr, w_gate, b_gate, rope_nat, rope_tr, *, batch, seq, tm, tk):
    rows, d = x2.shape
    tiles_per_batch = seq // tm
    const = lambda i: (0, 0)
    tile = lambda i: (i % tiles_per_batch, 0)
    tile_t = lambda i: (0, i % tiles_per_batch)
    bt = lambda i: (i // tiles_per_batch, 0, i % tiles_per_batch, 0)
    bt5 = lambda i: (i // tiles_per_batch, 0, i % tiles_per_batch, 0, 0)
    return pl.pallas_call(
        functools.partial(_inproj_kernel, tiles_per_batch=tiles_per_batch, tm=tm),
        grid=(rows // tm,),
        in_specs=[
            pl.BlockSpec((tm, d), lambda i: (i, 0)),
            pl.BlockSpec((1, d), const),
            pl.BlockSpec(w_nat.shape, const),
            pl.BlockSpec(w_tr.shape, const),
            pl.BlockSpec(w_gate.shape, const),
            pl.BlockSpec((1, LANES), const),
            pl.BlockSpec((tm, LANES), tile),
            pl.BlockSpec((tm, LANES), tile),
            pl.BlockSpec((tm, LANES), tile),
            pl.BlockSpec((SUBLANES, tm), tile_t),
            pl.BlockSpec((SUBLANES, tm), tile_t),
        ],
        out_specs=[
            pl.BlockSpec((None, QT_BLOCKS, None, LANES, tm), bt5),
            pl.BlockSpec((None, K_BLOCKS, tm, LANES), bt),
            pl.BlockSpec((None, VT_BLOCKS, tm // tk, VT_ROWS, tk), bt5),
        ],
        out_shape=[
            jax.ShapeDtypeStruct((batch, QT_BLOCKS, seq // tm, LANES, tm), BF16),
            jax.ShapeDtypeStruct((batch, K_BLOCKS, seq, LANES), BF16),
            jax.ShapeDtypeStruct((batch, VT_BLOCKS, seq // tk, VT_ROWS, tk), BF16),
        ],
        scratch_shapes=[pltpu.VMEM((SUBLANES, LANES), F32)],
        compiler_params=pltpu.CompilerParams(
            dimension_semantics=("arbitrary",), vmem_limit_bytes=VMEM_LIMIT),
        name="inproj",
    )(x2, g, w_nat, w_tr, w_gate, b_gate, *rope_nat, *rope_tr)


def _attn_kernel(*refs, kind, tq, tk, seq):
    if kind == "diff":
        lam_ref, g_ref, qt_ref, k_ref, vt_ref, o_ref = refs[:6]
        k_refs = (k_ref,)
    else:
        g_ref, qt0_ref, qt1_ref, k0_ref, k1_ref, vt_ref, o_ref = refs[:7]
        k_refs = (k0_ref, k1_ref)
    acc_scs, qs_scs, s_scs = refs[-6:-4], refs[-4:-2], refs[-2:]
    n = 2 * tq
    assert tq == tk
    rowq = lax.broadcasted_iota(jnp.int32, (LANES, tq), 0)

    def scores(t, qs_sc, s_ref, masked):
        w = n // len(k_refs)
        maxes = []
        for j, kr in enumerate(k_refs):
            s = jnp.dot(kr[t * tk:(t + 1) * tk, :], qs_sc[:, j * w:(j + 1) * w],
                        preferred_element_type=F32)
            if masked:
                key_l = lax.broadcasted_iota(jnp.int32, s.shape, 0)
                qry_l = lax.broadcasted_iota(jnp.int32, s.shape, 1) % tq
                s = jnp.where(key_l <= qry_l, s, NEG)
            s_ref[:, j * w:(j + 1) * w] = s
            maxes.append(jnp.max(s, axis=0, keepdims=True))
        return maxes[0] if len(maxes) == 1 else jnp.concatenate(maxes, axis=1)

    def absorb(s_ref, m_cur, t, ml, acc_sc):
        m, l = ml
        m_next = jnp.maximum(m, m_cur)
        alpha = jnp.exp2(m - m_next)
        p = jnp.exp2(s_ref[...] - m_next).astype(BF16)
        pv = jnp.dot(vt_ref[t], p, preferred_element_type=F32)
        acc_sc[...] = acc_sc[...] * alpha + pv[:LANES]
        return m_next, alpha * l + pv[LANES:LANES + 1]

    def fill_qs(qi, qs_sc):
        if kind == "diff":
            qt = qt_ref[qi]
            zero = jnp.zeros_like(qt)
            qs_sc[:, :tq] = jnp.where(rowq < QK_DIM, qt, zero)
            qs_sc[:, tq:] = jnp.where(rowq < QK_DIM, zero, qt)
        else:
            qs_sc[:, :tq] = qt0_ref[qi]
            qs_sc[:, tq:] = qt1_ref[qi]

    def finalize(qi, acc_sc, ml):
        m, l = ml
        acc = acc_sc[...]
        o0 = acc[:, :tq] / l[:, :tq]
        o1 = acc[:, tq:] / l[:, tq:]
        g = g_ref[...]
        if kind == "diff":
            lam = (jnp.exp(jnp.sum(lam_ref[0:1, :] * lam_ref[1:2, :], axis=1, keepdims=True))
                   - jnp.exp(jnp.sum(lam_ref[2:3, :] * lam_ref[3:4, :], axis=1, keepdims=True))
                   + LAMBDA_INIT)
            o = o0 - lam * o1
            ms = jnp.mean(o * o, axis=0, keepdims=True)
            o = o * lax.rsqrt(ms + SUBLN_EPS) * g * (1.0 - LAMBDA_INIT)
        else:
            halves = []
            for oh in (o0[:QK_DIM], o1[QK_DIM:]):
                ms = jnp.mean(oh * oh, axis=0, keepdims=True)
                halves.append(oh * lax.rsqrt(ms + EPS))
            o = jnp.concatenate(halves, axis=0) * g
        o_ref[qi * tq:(qi + 1) * tq, :] = o.T.astype(o_ref.dtype)

    steps = [(qi, t, t == qi) for qi in range(seq // tq) for t in [qi] + list(range(qi))]
    fill_qs(0, qs_scs[0])
    mc = scores(0, qs_scs[0], s_scs[0], True)
    ml = None
    for g, (qi, t, _) in enumerate(steps):
        acc_sc = acc_scs[qi % 2]
        if t == qi:
            acc_sc[...] = jnp.zeros(acc_sc.shape, F32)
            ml = (jnp.full((1, n), NEG, F32), jnp.zeros((1, n), F32))
        mc_next = None
        if g + 1 < len(steps):
            qi2, t2, masked2 = steps[g + 1]
            if qi2 != qi:
                fill_qs(qi2, qs_scs[qi2 % 2])
            mc_next = scores(t2, qs_scs[qi2 % 2], s_scs[(g + 1) % len(s_scs)], masked2)
        ml = absorb(s_scs[g % len(s_scs)], mc, t, ml, acc_sc)
        mc = mc_next
        if g + 1 == len(steps) or steps[g + 1][0] != qi:
            finalize(qi, acc_sc, ml)


def _attention(kind, qt, kn, vt, extra, *, tq, tk):
    batch, _, nq, _, _ = qt.shape
    seq = kn.shape[2]
    nk = vt.shape[2]
    const = lambda b, h: (0, 0)
    qspec = lambda f: pl.BlockSpec((None, None, nq, LANES, tq), lambda b, h: (b, f(h), 0, 0, 0))
    kspec = lambda f: pl.BlockSpec((None, None, seq, LANES), lambda b, h: (b, f(h), 0, 0))
    if kind == "diff":
        lam4, g = extra
        in_specs = [pl.BlockSpec(lam4.shape, const), pl.BlockSpec(g.shape, const),
                    qspec(lambda h: h), kspec(lambda h: h)]
        args = [lam4, g, qt, kn]
        v_base = 0
    else:
        (g,) = extra
        in_specs = [pl.BlockSpec(g.shape, const),
                    qspec(lambda h: N_DIFF_HEADS + 2 * h), qspec(lambda h: N_DIFF_HEADS + 2 * h + 1),
                    kspec(lambda h: N_DIFF_HEADS + 2 * h), kspec(lambda h: N_DIFF_HEADS + 2 * h + 1)]
        args = [g, qt, qt, kn, kn]
        v_base = N_DIFF_HEADS
    in_specs.append(pl.BlockSpec((None, None, nk, VT_ROWS, tk), lambda b, h: (b, v_base + h, 0, 0, 0)))
    args.append(vt)
    n_groups = 4
    return pl.pallas_call(
        functools.partial(_attn_kernel, kind=kind, tq=tq, tk=tk, seq=seq),
        grid=(batch, n_groups),
        in_specs=in_specs,
        out_specs=pl.BlockSpec((None, seq, LANES), lambda b, h: (b, 0, h)),
        out_shape=jax.ShapeDtypeStruct((batch, seq, n_groups * LANES), BF16),
        scratch_shapes=([pltpu.VMEM((LANES, 2 * tq), F32)] * 2
                        + [pltpu.VMEM((LANES, 2 * tq), BF16)] * 2
                        + [pltpu.VMEM((tk, 2 * tq), F32)] * 2),
        compiler_params=pltpu.CompilerParams(
            dimension_semantics=("arbitrary", "arbitrary"), vmem_limit_bytes=VMEM_LIMIT),
        name=kind + "_attn",
    )(*args)


def _memkv_kernel(mem_ref, g_ref, w_ref, kv_ref):
    u = _rms(mem_ref[...], g_ref[...], EPS).astype(BF16)
    kv_ref[...] = jnp.dot(u, w_ref[...], preferred_element_type=F32).astype(BF16)


def _memkv(mem2, g, w, *, tm):
    rows, d = mem2.shape
    n = w.shape[1]
    return pl.pallas_call(
        _memkv_kernel,
        grid=(rows // tm,),
        in_specs=[pl.BlockSpec((tm, d), lambda i: (i, 0)),
                  pl.BlockSpec((1, d), lambda i: (0, 0)),
                  pl.BlockSpec(w.shape, lambda i: (0, 0))],
        out_specs=pl.BlockSpec((tm, n), lambda i: (i, 0)),
        out_shape=jax.ShapeDtypeStruct((rows, n), BF16),
        compiler_params=pltpu.CompilerParams(
            dimension_semantics=("arbitrary",), vmem_limit_bytes=VMEM_LIMIT),
        name="memkv",
    )(mem2, g, w)


def _tail_kernel(x_ref, d_ref, f_ref, wo_ref, g_ref, wq_ref, kv_ref, wco_ref,
                 gm_ref, wu_ref, wd_ref, gf_ref, o_ref, *, fc):
    half = d_ref.shape[1]
    h1 = (x_ref[...]
          + jnp.dot(d_ref[...], wo_ref[:half, :], preferred_element_type=F32)
          + jnp.dot(f_ref[...], wo_ref[half:, :], preferred_element_type=F32))
    uq = _rms(h1, g_ref[...], EPS).astype(BF16)
    dm = h1.shape[1]
    hd = dm // N_CROSS_HEADS
    cq = (jnp.dot(uq, wq_ref[...], preferred_element_type=F32) * (hd ** -0.5)).astype(BF16)
    outs = []
    for hh in range(N_CROSS_HEADS):
        qh = cq[:, hh * hd:(hh + 1) * hd]
        kh = kv_ref[:, hh * hd:(hh + 1) * hd]
        vh = kv_ref[:, dm + hh * hd:dm + (hh + 1) * hd]
        s = lax.dot_general(qh, kh, (((1,), (1,)), ((), ())), preferred_element_type=F32)
        p = jnp.exp(s - jnp.max(s, axis=1, keepdims=True))
        l = jnp.sum(p, axis=1, keepdims=True)
        oh = jnp.dot(p.astype(BF16), vh, preferred_element_type=F32) / l
        outs.append(oh.astype(BF16))
    co = jnp.concatenate(outs, axis=1)
    h2 = h1 + jnp.dot(co, wco_ref[...], preferred_element_type=F32)

    u = _rms(h2, gm_ref[...], EPS).astype(BF16)
    acc = h2
    for c in range(wu_ref.shape[1] // fc):
        z = jnp.dot(u, wu_ref[:, c * fc:(c + 1) * fc], preferred_element_type=F32)
        a = jnp.square(jnp.maximum(z, 0.0)).astype(BF16)
        acc = acc + jnp.dot(a, wd_ref[c * fc:(c + 1) * fc, :], preferred_element_type=F32)
    o_ref[...] = _rms(acc, gf_ref[...], EPS)


def _tail(x2, d2, f2, w_out, g_cross, w_cq, kv, w_co, g_mlp, w_up, w_down, g_final, *, seq, tm, fc):
    rows, dm = x2.shape
    tiles_per_batch = seq // tm
    n_mem = kv.shape[1]
    const = lambda i: (0, 0)
    resident = lambda w: pl.BlockSpec(w.shape, const, pipeline_mode=pl.Buffered(1))
    return pl.pallas_call(
        functools.partial(_tail_kernel, fc=fc),
        grid=(rows // tm,),
        in_specs=[
            pl.BlockSpec((tm, dm), lambda i: (i, 0)),
            pl.BlockSpec((tm, d2.shape[1]), lambda i: (i, 0)),
            pl.BlockSpec((tm, f2.shape[1]), lambda i: (i, 0)),
            resident(w_out),
            pl.BlockSpec((1, dm), const),
            resident(w_cq),
            pl.BlockSpec((None, n_mem, kv.shape[2]), lambda i: (i // tiles_per_batch, 0, 0)),
            resident(w_co),
            pl.BlockSpec((1, dm), const),
            resident(w_up),
            resident(w_down),
            pl.BlockSpec((1, dm), const),
        ],
        out_specs=pl.BlockSpec((tm, dm), lambda i: (i, 0)),
        out_shape=jax.ShapeDtypeStruct((rows, dm), F32),
        compiler_params=pltpu.CompilerParams(
            dimension_semantics=("arbitrary",), vmem_limit_bytes=VMEM_LIMIT),
        name="tail",
    )(x2, d2, f2, w_out, g_cross, w_cq, kv, w_co, g_mlp, w_up, w_down, g_final)


def _rope_tables(seq):
    pos = jnp.arange(seq, dtype=F32)
    inv_freq = ROPE_THETA ** (-jnp.arange(0, ROT_DIM, 2, dtype=F32) / ROT_DIM)
    ang = pos[:, None] * inv_freq[None, :]
    cos, sin = jnp.cos(ang), jnp.sin(ang)
    lane = jnp.arange(LANES)
    l64 = lane % QK_DIM
    fidx = l64 % (ROT_DIM // 2)
    in_rot = (l64 < ROT_DIM)[None, :]
    first = (l64 < ROT_DIM // 2)[None, :]
    rc = jnp.where(in_rot, cos[:, fidx], 1.0)
    ra = jnp.where(in_rot & ~first, sin[:, fidx], 0.0)
    rb = jnp.where(first, -sin[:, fidx], 0.0)
    return (rc.astype(F32), ra.astype(F32), rb.astype(F32)), (cos.T.astype(F32), sin.T.astype(F32))


def _gate_lane_layout(cols):
    odd = jnp.repeat(cols[..., 1::2], SUBLANES, axis=-1)
    even = jnp.repeat(cols[..., 0::2], SUBLANES, axis=-1)
    pad = jnp.zeros(cols.shape[:-1] + (QK_DIM - odd.shape[-1],), cols.dtype)
    return jnp.concatenate([odd, pad, even, pad], axis=-1)


def kernel(x, mem, norm_mix_g, w_in, b_forget, lam_q1, lam_k1, lam_q2, lam_k2,
           diff_subln_g, fox_out_g, w_out, norm_cross_g, norm_mem_g, w_cq, w_ckv, w_co,
           norm_mlp_g, w_up, w_down, norm_final_g):
    batch, seq, dm = x.shape
    n_mem = mem.shape[1]
    assert norm_mix_g.shape[0] == 1, "single-layer block"
    assert w_in.shape[2] == 6 * 512 + N_FOX_HEADS

    x2 = x.reshape(batch * seq, dm)
    w = w_in[0]
    dq, dk, dv, fq, fk, fv = (w[:, c * 512:(c + 1) * 512] for c in range(6))
    w_nat = jnp.concatenate([dk, fk], axis=1).astype(BF16)
    w_tr = jnp.concatenate([dq, fq, dv, fv], axis=1).T.astype(BF16)
    w_gate = _gate_lane_layout(w[:, 6 * 512:]).astype(BF16)
    b_gate = _gate_lane_layout(b_forget[0]).reshape(1, LANES)
    rope_nat, rope_tr = _rope_tables(seq)

    qt, kn, vt = _inproj(x2, norm_mix_g[0].reshape(1, dm), w_nat, w_tr, w_gate, b_gate,
                         rope_nat, rope_tr, batch=batch, seq=seq, tm=ROW_TILE, tk=ATTN_TK)

    lam4 = jnp.stack([lam_q1[0], lam_k1[0], lam_q2[0], lam_k2[0]], axis=0)
    d_out = _attention("diff", qt, kn, vt, (lam4, diff_subln_g[0].reshape(LANES, 1)),
                       tq=ATTN_TQ, tk=ATTN_TK)
    f_out = _attention("fox", qt, kn, vt, (jnp.tile(fox_out_g[0], 2).reshape(LANES, 1),),
                       tq=ATTN_TQ, tk=ATTN_TK)

    kv = _memkv(mem.reshape(batch * n_mem, dm), norm_mem_g[0].reshape(1, dm),
                w_ckv[0].astype(BF16), tm=ROW_TILE)
    out = _tail(x2, d_out.reshape(batch * seq, -1), f_out.reshape(batch * seq, -1),
                w_out[0].astype(BF16), norm_cross_g[0].reshape(1, dm), w_cq[0].astype(BF16),
                kv.reshape(batch, n_mem, -1), w_co[0].astype(BF16),
                norm_mlp_g[0].reshape(1, dm), w_up[0].astype(BF16), w_down[0].astype(BF16),
                norm_final_g.reshape(1, dm), seq=seq, tm=ROW_TILE, fc=MLP_FF_CHUNK)
    return out.reshape(batch, seq, dm)
```

```python
import functools

import jax
import jax.numpy as jnp
from jax import lax
from jax.experimental import pallas as pl
from jax.experimental.pallas import tpu as pltpu

F32 = jnp.float32
BF16 = jnp.bfloat16

LANES = 128
SUBLANES = 8
MXU_COLS = 256
QK_DIM = 64
ROT_DIM = 16
ROPE_THETA = 500000.0
N_DIFF_HEADS = 4
N_FOX_HEADS = 8
N_FOX_PAIRS = N_FOX_HEADS // 2
N_CROSS_HEADS = 4
EPS = 1e-6
SUBLN_EPS = 1e-5
LAMBDA_INIT = 0.8 - 0.6 * 1.0
NEG = -1e30
LOG2E = 1.4426950408889634

VMEM_LIMIT = 56 * 1024 * 1024

ROW_TILE = 512
ATTN_TQ = 512
ATTN_TK = 512
MLP_FF_CHUNK = 1024

QT_BLOCKS = N_DIFF_HEADS + N_FOX_HEADS
K_BLOCKS = N_DIFF_HEADS + N_FOX_HEADS
VT_BLOCKS = N_DIFF_HEADS + N_FOX_PAIRS
ONES_ROWS = 16
VT_ROWS = LANES + ONES_ROWS


def _rms(x, g, eps):
    ms = jnp.mean(x * x, axis=-1, keepdims=True)
    return x * lax.rsqrt(ms + eps) * g


def _bf16_pieces(c):
    hi = c.astype(BF16).astype(F32)
    r1 = c - hi
    mid = r1.astype(BF16).astype(F32)
    lo = (r1 - mid).astype(BF16).astype(F32)
    return hi, mid, lo


def _inproj_kernel(x_ref, g_ref, wn_ref, wt_ref, wg_ref, bg_ref, rc_ref, ra_ref, rb_ref,
                   ct_ref, st_ref, qt_ref, k_ref, vt_ref, carry_ref, *, tiles_per_batch, tm):
    i = pl.program_id(0)
    u = _rms(x_ref[...], g_ref[...], EPS).astype(BF16)
    scale = QK_DIM ** -0.5 * LOG2E

    gate = jnp.dot(u, wg_ref[...], preferred_element_type=F32) + bg_ref[...]
    logf = jnp.minimum(gate, 0.0) - jnp.log1p(jnp.exp(-jnp.abs(gate)))

    @pl.when(i % tiles_per_batch == 0)
    def _():
        carry_ref[...] = jnp.zeros_like(carry_ref)

    sub = 256
    row = lax.broadcasted_iota(jnp.int32, (sub, sub), 0)
    col = lax.broadcasted_iota(jnp.int32, (sub, sub), 1)
    tri = (row >= col).astype(BF16)
    carry = carry_ref[0:1, :]
    parts = []
    for r in range(tm // sub):
        hi, mid, lo = _bf16_pieces(logf[r * sub:(r + 1) * sub])
        cs = (jnp.dot(tri, hi.astype(BF16), preferred_element_type=F32)
              + jnp.dot(tri, mid.astype(BF16), preferred_element_type=F32)
              + jnp.dot(tri, lo.astype(BF16), preferred_element_type=F32)) + carry
        carry = cs[sub - 1:sub, :]
        parts.append(cs)
    carry_ref[...] = jnp.broadcast_to(carry, carry_ref.shape)
    cs = jnp.concatenate(parts, axis=0) * LOG2E
    cst = cs.T

    rc, ra, rb = rc_ref[...], ra_ref[...], rb_ref[...]
    lane = lax.broadcasted_iota(jnp.int32, (tm, LANES), 1)
    low = lane < QK_DIM
    l64 = lane % QK_DIM
    yn = jnp.dot(u, wn_ref[...], preferred_element_type=F32)
    for h in range(N_DIFF_HEADS):
        t = yn[:, h * LANES:(h + 1) * LANES]
        t = t * rc + pltpu.roll(t, 8, 1) * ra + pltpu.roll(t, LANES - 8, 1) * rb
        k_ref[h] = t.astype(BF16)
    for p in range(N_FOX_PAIRS):
        yp = yn[:, (N_DIFF_HEADS + p) * LANES:(N_DIFF_HEADS + p + 1) * LANES]
        rolled = cs if p == 0 else pltpu.roll(cs, LANES - 8 * p, 1)
        hi, mid, lo = _bf16_pieces(rolled)
        augk = jnp.where(l64 == 0, -hi, jnp.where(l64 == 1, -mid, jnp.where(
            l64 == 2, -lo, jnp.where(l64 < 6, 1.0, 0.0))))
        k_ref[N_DIFF_HEADS + 2 * p] = jnp.where(low, yp, augk).astype(BF16)
        k_ref[N_DIFF_HEADS + 2 * p + 1] = jnp.where(low, augk, yp).astype(BF16)

    ct, st = ct_ref[...], st_ref[...]
    nt = (((1,), (1,)), ((), ()))
    rowi = lax.broadcasted_iota(jnp.int32, (SUBLANES, tm), 0)
    zeros56 = jnp.zeros((QK_DIM - SUBLANES, tm), F32)

    yt = lax.dot_general(wt_ref[0:512, :], u, nt, preferred_element_type=F32)
    for h in range(N_DIFF_HEADS):
        tiles = [yt[h * LANES + r:h * LANES + r + SUBLANES] for r in range(0, LANES, SUBLANES)]
        for base in (0, QK_DIM // SUBLANES):
            x1, x2 = tiles[base], tiles[base + 1]
            tiles[base] = x1 * ct - x2 * st
            tiles[base + 1] = x2 * ct + x1 * st
        qt_ref[h] = (jnp.concatenate(tiles, axis=0) * scale).astype(BF16)

    yt = lax.dot_general(wt_ref[512:1024, :], u, nt, preferred_element_type=F32)
    for h in range(N_FOX_HEADS):
        p, e = divmod(h, 2)
        src = (QK_DIM if e == 0 else 0) + SUBLANES * p
        hi, mid, lo = _bf16_pieces(cst[src:src + SUBLANES])
        aug8 = jnp.where(rowi < 3, 1.0, jnp.where(rowi == 3, hi, jnp.where(
            rowi == 4, mid, jnp.where(rowi == 5, lo, 0.0))))
        qh = yt[h * QK_DIM:(h + 1) * QK_DIM] * scale
        blocks = [qh, aug8, zeros56] if e == 0 else [aug8, zeros56, qh]
        qt_ref[N_DIFF_HEADS + h] = jnp.concatenate(blocks, axis=0).astype(BF16)

    tk = vt_ref.shape[-1]
    for c in range(2):
        yt = lax.dot_general(wt_ref[1024 + 512 * c:1536 + 512 * c, :], u, nt,
                             preferred_element_type=F32)
        for j in range(4):
            blk = jnp.concatenate([yt[j * LANES:(j + 1) * LANES], jnp.ones((ONES_ROWS, tm), F32)],
                                  axis=0).astype(BF16)
            for t in range(tm // tk):
                vt_ref[4 * c + j, t] = blk[:, t * tk:(t + 1) * tk]


def _inproj(x2, g, w_nat, w_tr, w_gate, b_gate, rope_nat, rope_tr, *, batch, seq, tm, tk):
    rows, d = x2.shape
    tiles_per_batch = seq // tm
    const = lambda i: (0, 0)
    tile = lambda i: (i % tiles_per_batch, 0)
    tile_t = lambda i: (0, i % tiles_per_batch)
    bt = lambda i: (i // tiles_per_batch, 0, i % tiles_per_batch, 0)
    bt5 = lambda i: (i // tiles_per_batch, 0, i % tiles_per_batch, 0, 0)
    return pl.pallas_call(
        functools.partial(_inproj_kernel, tiles_per_batch=tiles_per_batch, tm=tm),
        grid=(rows // tm,),
        in_specs=[
            pl.BlockSpec((tm, d), lambda i: (i, 0)),
            pl.BlockSpec((1, d), const),
            pl.BlockSpec(w_nat.shape, const),
            pl.BlockSpec(w_tr.shape, const),
            pl.BlockSpec(w_gate.shape, const),
            pl.BlockSpec((1, LANES), const),
            pl.BlockSpec((tm, LANES), tile),
            pl.BlockSpec((tm, LANES), tile),
            pl.BlockSpec((tm, LANES), tile),
            pl.BlockSpec((SUBLANES, tm), tile_t),
            pl.BlockSpec((SUBLANES, tm), tile_t),
        ],
        out_specs=[
            pl.BlockSpec((None, QT_BLOCKS, None, LANES, tm), bt5),
            pl.BlockSpec((None, K_BLOCKS, tm, LANES), bt),
            pl.BlockSpec((None, VT_BLOCKS, tm // tk, VT_ROWS, tk), bt5),
        ],
        out_shape=[
            jax.ShapeDtypeStruct((batch, QT_BLOCKS, seq // tm, LANES, tm), BF16),
            jax.ShapeDtypeStruct((batch, K_BLOCKS, seq, LANES), BF16),
            jax.ShapeDtypeStruct((batch, VT_BLOCKS, seq // tk, VT_ROWS, tk), BF16),
        ],
        scratch_shapes=[pltpu.VMEM((SUBLANES, LANES), F32)],
        compiler_params=pltpu.CompilerParams(
            dimension_semantics=("arbitrary",), vmem_limit_bytes=VMEM_LIMIT),
        name="inproj",
    )(x2, g, w_nat, w_tr, w_gate, b_gate, *rope_nat, *rope_tr)


def _attn_kernel(*refs, kind, tq, tk, seq):
    if kind == "diff":
        lam_ref, g_ref, qt_ref, k_ref, vt_ref, o_ref = refs[:6]
        k_refs = (k_ref,)
    else:
        g_ref, qt0_ref, qt1_ref, k0_ref, k1_ref, vt_ref, o_ref = refs[:7]
        k_refs = (k0_ref, k1_ref)
    acc_scs, qs_scs, s_scs = refs[-6:-4], refs[-4:-2], refs[-2:]
    n = 2 * tq
    assert tq == tk
    rowq = lax.broadcasted_iota(jnp.int32, (LANES, tq), 0)

    sw = MXU_COLS
    n_strips = n // sw
    strips_per_group = tq // sw

    def strip_keys(st, masked):
        return (st % strips_per_group + 1) * sw if masked else tk

    def scores(t, qs_sc, s_ref, masked):
        maxes = []
        if masked:
            for st in range(n_strips):
                nk = strip_keys(st, True)
                kr = k_refs[st * len(k_refs) // n_strips]
                s = jnp.dot(kr[t * tk:t * tk + nk, :], qs_sc[:, st * sw:(st + 1) * sw],
                            preferred_element_type=F32)
                key_l = lax.broadcasted_iota(jnp.int32, s.shape, 0)
                qry_l = lax.broadcasted_iota(jnp.int32, s.shape, 1) + (st % strips_per_group) * sw
                s = jnp.where(key_l <= qry_l, s, NEG)
                s_ref[:nk, st * sw:(st + 1) * sw] = s
                maxes.append(jnp.max(s, axis=0, keepdims=True))
        else:
            w = n // len(k_refs)
            for j, kr in enumerate(k_refs):
                s = jnp.dot(kr[t * tk:(t + 1) * tk, :], qs_sc[:, j * w:(j + 1) * w],
                            preferred_element_type=F32)
                s_ref[:, j * w:(j + 1) * w] = s
                maxes.append(jnp.max(s, axis=0, keepdims=True))
        return maxes[0] if len(maxes) == 1 else jnp.concatenate(maxes, axis=1)

    def absorb(s_ref, m_cur, t, ml, acc_sc, masked):
        m, l = ml
        m_next = jnp.maximum(m, m_cur)
        alpha = jnp.exp2(m - m_next)
        sums = []
        width = sw if masked else n
        for st in range(n // width):
            nk = strip_keys(st, masked)
            sl = slice(st * width, (st + 1) * width)
            p = jnp.exp2(s_ref[:nk, sl] - m_next[:, sl]).astype(BF16)
            pv = jnp.dot(vt_ref[t, :, :nk], p, preferred_element_type=F32)
            acc_sc[:, sl] = acc_sc[:, sl] * alpha[:, sl] + pv[:LANES]
            sums.append(pv[LANES:LANES + 1])
        return m_next, alpha * l + jnp.concatenate(sums, axis=1)

    def fill_qs(qi, qs_sc):
        if kind == "diff":
            qt = qt_ref[qi]
            zero = jnp.zeros_like(qt)
            qs_sc[:, :tq] = jnp.where(rowq < QK_DIM, qt, zero)
            qs_sc[:, tq:] = jnp.where(rowq < QK_DIM, zero, qt)
        else:
            qs_sc[:, :tq] = qt0_ref[qi]
            qs_sc[:, tq:] = qt1_ref[qi]

    def finalize(qi, acc_sc, ml):
        m, l = ml
        acc = acc_sc[...]
        o0 = acc[:, :tq] / l[:, :tq]
        o1 = acc[:, tq:] / l[:, tq:]
        g = g_ref[...]
        if kind == "diff":
            lam = (jnp.exp(jnp.sum(lam_ref[0:1, :] * lam_ref[1:2, :], axis=1, keepdims=True))
                   - jnp.exp(jnp.sum(lam_ref[2:3, :] * lam_ref[3:4, :], axis=1, keepdims=True))
                   + LAMBDA_INIT)
            o = o0 - lam * o1
            ms = jnp.mean(o * o, axis=0, keepdims=True)
            o = o * lax.rsqrt(ms + SUBLN_EPS) * g * (1.0 - LAMBDA_INIT)
        else:
            halves = []
            for oh in (o0[:QK_DIM], o1[QK_DIM:]):
                ms = jnp.mean(oh * oh, axis=0, keepdims=True)
                halves.append(oh * lax.rsqrt(ms + EPS))
            o = jnp.concatenate(halves, axis=0) * g
        o_ref[qi * tq:(qi + 1) * tq, :] = o.T.astype(o_ref.dtype)

    steps = [(qi, t, t == qi) for qi in range(seq // tq) for t in [qi] + list(range(qi))]
    fill_qs(0, qs_scs[0])
    mc = scores(0, qs_scs[0], s_scs[0], True)
    ml = None
    for g, (qi, t, _) in enumerate(steps):
        acc_sc = acc_scs[qi % 2]
        if t == qi:
            acc_sc[...] = jnp.zeros(acc_sc.shape, F32)
            ml = (jnp.full((1, n), NEG, F32), jnp.zeros((1, n), F32))
        mc_next = None
        if g + 1 < len(steps):
            qi2, t2, masked2 = steps[g + 1]
            if qi2 != qi:
                fill_qs(qi2, qs_scs[qi2 % 2])
            mc_next = scores(t2, qs_scs[qi2 % 2], s_scs[(g + 1) % len(s_scs)], masked2)
        ml = absorb(s_scs[g % len(s_scs)], mc, t, ml, acc_sc, t == qi)
        mc = mc_next
        if g + 1 == len(steps) or steps[g + 1][0] != qi:
            finalize(qi, acc_sc, ml)


def _attention(kind, qt, kn, vt, extra, *, tq, tk):
    batch, _, nq, _, _ = qt.shape
    seq = kn.shape[2]
    nk = vt.shape[2]
    const = lambda b, h: (0, 0)
    qspec = lambda f: pl.BlockSpec((None, None, nq, LANES, tq), lambda b, h: (b, f(h), 0, 0, 0))
    kspec = lambda f: pl.BlockSpec((None, None, seq, LANES), lambda b, h: (b, f(h), 0, 0))
    if kind == "diff":
        lam4, g = extra
        in_specs = [pl.BlockSpec(lam4.shape, const), pl.BlockSpec(g.shape, const),
                    qspec(lambda h: h), kspec(lambda h: h)]
        args = [lam4, g, qt, kn]
        v_base = 0
    else:
        (g,) = extra
        in_specs = [pl.BlockSpec(g.shape, const),
                    qspec(lambda h: N_DIFF_HEADS + 2 * h), qspec(lambda h: N_DIFF_HEADS + 2 * h + 1),
                    kspec(lambda h: N_DIFF_HEADS + 2 * h), kspec(lambda h: N_DIFF_HEADS + 2 * h + 1)]
        args = [g, qt, qt, kn, kn]
        v_base = N_DIFF_HEADS
    in_specs.append(pl.BlockSpec((None, None, nk, VT_ROWS, tk), lambda b, h: (b, v_base + h, 0, 0, 0)))
    args.append(vt)
    n_groups = 4
    return pl.pallas_call(
        functools.partial(_attn_kernel, kind=kind, tq=tq, tk=tk, seq=seq),
        grid=(batch, n_groups),
        in_specs=in_specs,
        out_specs=pl.BlockSpec((None, seq, LANES), lambda b, h: (b, 0, h)),
        out_shape=jax.ShapeDtypeStruct((batch, seq, n_groups * LANES), BF16),
        scratch_shapes=([pltpu.VMEM((LANES, 2 * tq), F32)] * 2
                        + [pltpu.VMEM((LANES, 2 * tq), BF16)] * 2
                        + [pltpu.VMEM((tk, 2 * tq), F32)] * 2),
        compiler_params=pltpu.CompilerParams(
            dimension_semantics=("arbitrary", "arbitrary"), vmem_limit_bytes=VMEM_LIMIT),
        name=kind + "_attn",
    )(*args)


def _memkv_kernel(mem_ref, g_ref, w_ref, kv_ref):
    u = _rms(mem_ref[...], g_ref[...], EPS).astype(BF16)
    kv_ref[...] = jnp.dot(u, w_ref[...], preferred_element_type=F32).astype(BF16)


def _memkv(mem2, g, w, *, tm):
    rows, d = mem2.shape
    n = w.shape[1]
    return pl.pallas_call(
        _memkv_kernel,
        grid=(rows // tm,),
        in_specs=[pl.BlockSpec((tm, d), lambda i: (i, 0)),
                  pl.BlockSpec((1, d), lambda i: (0, 0)),
                  pl.BlockSpec(w.shape, lambda i: (0, 0))],
        out_specs=pl.BlockSpec((tm, n), lambda i: (i, 0)),
        out_shape=jax.ShapeDtypeStruct((rows, n), BF16),
        compiler_params=pltpu.CompilerParams(
            dimension_semantics=("arbitrary",), vmem_limit_bytes=VMEM_LIMIT),
        name="memkv",
    )(mem2, g, w)


def _tail_kernel(x_ref, d_ref, f_ref, wo_ref, g_ref, wq_ref, kv_ref, wco_ref,
                 gm_ref, wu_ref, wd_ref, gf_ref, o_ref, *, fc):
    half = d_ref.shape[1]
    h1 = (x_ref[...]
          + jnp.dot(d_ref[...], wo_ref[:half, :], preferred_element_type=F32)
          + jnp.dot(f_ref[...], wo_ref[half:, :], preferred_element_type=F32))
    uq = _rms(h1, g_ref[...], EPS).astype(BF16)
    dm = h1.shape[1]
    hd = dm // N_CROSS_HEADS
    cq = (jnp.dot(uq, wq_ref[...], preferred_element_type=F32) * (hd ** -0.5)).astype(BF16)
    outs = []
    for hh in range(N_CROSS_HEADS):
        qh = cq[:, hh * hd:(hh + 1) * hd]
        kh = kv_ref[:, hh * hd:(hh + 1) * hd]
        vh = kv_ref[:, dm + hh * hd:dm + (hh + 1) * hd]
        s = lax.dot_general(qh, kh, (((1,), (1,)), ((), ())), preferred_element_type=F32)
        p = jnp.exp(s - jnp.max(s, axis=1, keepdims=True))
        l = jnp.sum(p, axis=1, keepdims=True)
        oh = jnp.dot(p.astype(BF16), vh, preferred_element_type=F32) / l
        outs.append(oh.astype(BF16))
    co = jnp.concatenate(outs, axis=1)
    h2 = h1 + jnp.dot(co, wco_ref[...], preferred_element_type=F32)

    u = _rms(h2, gm_ref[...], EPS).astype(BF16)
    acc = h2
    for c in range(wu_ref.shape[1] // fc):
        z = jnp.dot(u, wu_ref[:, c * fc:(c + 1) * fc], preferred_element_type=F32)
        a = jnp.square(jnp.maximum(z, 0.0)).astype(BF16)
        acc = acc + jnp.dot(a, wd_ref[c * fc:(c + 1) * fc, :], preferred_element_type=F32)
    o_ref[...] = _rms(acc, gf_ref[...], EPS)


def _tail(x2, d2, f2, w_out, g_cross, w_cq, kv, w_co, g_mlp, w_up, w_down, g_final, *, seq, tm, fc):
    rows, dm = x2.shape
    tiles_per_batch = seq // tm
    n_mem = kv.shape[1]
    const = lambda i: (0, 0)
    resident = lambda w: pl.BlockSpec(w.shape, const, pipeline_mode=pl.Buffered(1))
    return pl.pallas_call(
        functools.partial(_tail_kernel, fc=fc),
        grid=(rows // tm,),
        in_specs=[
            pl.BlockSpec((tm, dm), lambda i: (i, 0)),
            pl.BlockSpec((tm, d2.shape[1]), lambda i: (i, 0)),
            pl.BlockSpec((tm, f2.shape[1]), lambda i: (i, 0)),
            resident(w_out),
            pl.BlockSpec((1, dm), const),
            resident(w_cq),
            pl.BlockSpec((None, n_mem, kv.shape[2]), lambda i: (i // tiles_per_batch, 0, 0)),
            resident(w_co),
            pl.BlockSpec((1, dm), const),
            resident(w_up),
            resident(w_down),
            pl.BlockSpec((1, dm), const),
        ],
        out_specs=pl.BlockSpec((tm, dm), lambda i: (i, 0)),
        out_shape=jax.ShapeDtypeStruct((rows, dm), F32),
        compiler_params=pltpu.CompilerParams(
            dimension_semantics=("arbitrary",), vmem_limit_bytes=VMEM_LIMIT),
        name="tail",
    )(x2, d2, f2, w_out, g_cross, w_cq, kv, w_co, g_mlp, w_up, w_down, g_final)


def _rope_tables(seq):
    pos = jnp.arange(seq, dtype=F32)
    inv_freq = ROPE_THETA ** (-jnp.arange(0, ROT_DIM, 2, dtype=F32) / ROT_DIM)
    ang = pos[:, None] * inv_freq[None, :]
    cos, sin = jnp.cos(ang), jnp.sin(ang)
    lane = jnp.arange(LANES)
    l64 = lane % QK_DIM
    fidx = l64 % (ROT_DIM // 2)
    in_rot = (l64 < ROT_DIM)[None, :]
    first = (l64 < ROT_DIM // 2)[None, :]
    rc = jnp.where(in_rot, cos[:, fidx], 1.0)
    ra = jnp.where(in_rot & ~first, sin[:, fidx], 0.0)
    rb = jnp.where(first, -sin[:, fidx], 0.0)
    return (rc.astype(F32), ra.astype(F32), rb.astype(F32)), (cos.T.astype(F32), sin.T.astype(F32))


def _gate_lane_layout(cols):
    odd = jnp.repeat(cols[..., 1::2], SUBLANES, axis=-1)
    even = jnp.repeat(cols[..., 0::2], SUBLANES, axis=-1)
    pad = jnp.zeros(cols.shape[:-1] + (QK_DIM - odd.shape[-1],), cols.dtype)
    return jnp.concatenate([odd, pad, even, pad], axis=-1)


def kernel(x, mem, norm_mix_g, w_in, b_forget, lam_q1, lam_k1, lam_q2, lam_k2,
           diff_subln_g, fox_out_g, w_out, norm_cross_g, norm_mem_g, w_cq, w_ckv, w_co,
           norm_mlp_g, w_up, w_down, norm_final_g):
    batch, seq, dm = x.shape
    n_mem = mem.shape[1]
    assert norm_mix_g.shape[0] == 1, "single-layer block"
    assert w_in.shape[2] == 6 * 512 + N_FOX_HEADS

    x2 = x.reshape(batch * seq, dm)
    w = w_in[0]
    dq, dk, dv, fq, fk, fv = (w[:, c * 512:(c + 1) * 512] for c in range(6))
    w_nat = jnp.concatenate([dk, fk], axis=1).astype(BF16)
    w_tr = jnp.concatenate([dq, fq, dv, fv], axis=1).T.astype(BF16)
    w_gate = _gate_lane_layout(w[:, 6 * 512:]).astype(BF16)
    b_gate = _gate_lane_layout(b_forget[0]).reshape(1, LANES)
    rope_nat, rope_tr = _rope_tables(seq)

    qt, kn, vt = _inproj(x2, norm_mix_g[0].reshape(1, dm), w_nat, w_tr, w_gate, b_gate,
                         rope_nat, rope_tr, batch=batch, seq=seq, tm=ROW_TILE, tk=ATTN_TK)

    lam4 = jnp.stack([lam_q1[0], lam_k1[0], lam_q2[0], lam_k2[0]], axis=0)
    d_out = _attention("diff", qt, kn, vt, (lam4, diff_subln_g[0].reshape(LANES, 1)),
                       tq=ATTN_TQ, tk=ATTN_TK)
    f_out = _attention("fox", qt, kn, vt, (jnp.tile(fox_out_g[0], 2).reshape(LANES, 1),),
                       tq=ATTN_TQ, tk=ATTN_TK)

    kv = _memkv(mem.reshape(batch * n_mem, dm), norm_mem_g[0].reshape(1, dm),
                w_ckv[0].astype(BF16), tm=ROW_TILE)
    out = _tail(x2, d_out.reshape(batch * seq, -1), f_out.reshape(batch * seq, -1),
                w_out[0].astype(BF16), norm_cross_g[0].reshape(1, dm), w_cq[0].astype(BF16),
                kv.reshape(batch, n_mem, -1), w_co[0].astype(BF16),
                norm_mlp_g[0].reshape(1, dm), w_up[0].astype(BF16), w_down[0].astype(BF16),
                norm_final_g.reshape(1, dm), seq=seq, tm=ROW_TILE, fc=MLP_FF_CHUNK)
    return out.reshape(batch, seq, dm)
```

```python
import functools

import jax
import jax.numpy as jnp
from jax import lax
from jax.experimental import pallas as pl
from jax.experimental.pallas import tpu as pltpu

F32 = jnp.float32
BF16 = jnp.bfloat16

LANES = 128
SUBLANES = 8
MXU_COLS = 256
QK_DIM = 64
ROT_DIM = 16
ROPE_THETA = 500000.0
N_DIFF_HEADS = 4
N_FOX_HEADS = 8
N_FOX_PAIRS = N_FOX_HEADS // 2
N_CROSS_HEADS = 4
EPS = 1e-6
SUBLN_EPS = 1e-5
LAMBDA_INIT = 0.8 - 0.6 * 1.0
NEG = -1e30
LOG2E = 1.4426950408889634

VMEM_LIMIT = 56 * 1024 * 1024

ROW_TILE = 512
ATTN_TQ = 512
ATTN_TK = 512
MLP_FF_CHUNK = 1024
STEPS_PER_REGION = 12

QT_BLOCKS = N_DIFF_HEADS + N_FOX_HEADS
K_BLOCKS = N_DIFF_HEADS + N_FOX_HEADS
VT_BLOCKS = N_DIFF_HEADS + N_FOX_PAIRS
ONES_ROWS = 16
VT_ROWS = LANES + ONES_ROWS


def _rms(x, g, eps):
    ms = jnp.mean(x * x, axis=-1, keepdims=True)
    return x * lax.rsqrt(ms + eps) * g


def _bf16_pieces(c):
    hi = c.astype(BF16).astype(F32)
    r1 = c - hi
    mid = r1.astype(BF16).astype(F32)
    lo = (r1 - mid).astype(BF16).astype(F32)
    return hi, mid, lo


def _inproj_kernel(x_ref, g_ref, wn_ref, wt_ref, wg_ref, bg_ref, rc_ref, ra_ref, rb_ref,
                   ct_ref, st_ref, qt_ref, k_ref, vt_ref, carry_ref, *, tiles_per_batch, tm):
    i = pl.program_id(0)
    u = _rms(x_ref[...], g_ref[...], EPS).astype(BF16)
    scale = QK_DIM ** -0.5 * LOG2E

    gate = jnp.dot(u, wg_ref[...], preferred_element_type=F32) + bg_ref[...]
    logf = jnp.minimum(gate, 0.0) - jnp.log1p(jnp.exp(-jnp.abs(gate)))

    @pl.when(i % tiles_per_batch == 0)
    def _():
        carry_ref[...] = jnp.zeros_like(carry_ref)

    sub = 256
    row = lax.broadcasted_iota(jnp.int32, (sub, sub), 0)
    col = lax.broadcasted_iota(jnp.int32, (sub, sub), 1)
    tri = (row >= col).astype(BF16)
    carry = carry_ref[0:1, :]
    parts = []
    for r in range(tm // sub):
        hi, mid, lo = _bf16_pieces(logf[r * sub:(r + 1) * sub])
        cs = (jnp.dot(tri, hi.astype(BF16), preferred_element_type=F32)
              + jnp.dot(tri, mid.astype(BF16), preferred_element_type=F32)
              + jnp.dot(tri, lo.astype(BF16), preferred_element_type=F32)) + carry
        carry = cs[sub - 1:sub, :]
        parts.append(cs)
    carry_ref[...] = jnp.broadcast_to(carry, carry_ref.shape)
    cs = jnp.concatenate(parts, axis=0) * LOG2E
    cst = cs.T

    rc, ra, rb = rc_ref[...], ra_ref[...], rb_ref[...]
    lane = lax.broadcasted_iota(jnp.int32, (tm, LANES), 1)
    low = lane < QK_DIM
    l64 = lane % QK_DIM
    yn = jnp.dot(u, wn_ref[...], preferred_element_type=F32)
    for h in range(N_DIFF_HEADS):
        t = yn[:, h * LANES:(h + 1) * LANES]
        t = t * rc + pltpu.roll(t, 8, 1) * ra + pltpu.roll(t, LANES - 8, 1) * rb
        k_ref[h] = t.astype(BF16)
    for p in range(N_FOX_PAIRS):
        yp = yn[:, (N_DIFF_HEADS + p) * LANES:(N_DIFF_HEADS + p + 1) * LANES]
        rolled = cs if p == 0 else pltpu.roll(cs, LANES - 8 * p, 1)
        hi, mid, lo = _bf16_pieces(rolled)
        augk = jnp.where(l64 == 0, -hi, jnp.where(l64 == 1, -mid, jnp.where(
            l64 == 2, -lo, jnp.where(l64 < 6, 1.0, 0.0))))
        k_ref[N_DIFF_HEADS + 2 * p] = jnp.where(low, yp, augk).astype(BF16)
        k_ref[N_DIFF_HEADS + 2 * p + 1] = jnp.where(low, augk, yp).astype(BF16)

    ct, st = ct_ref[...], st_ref[...]
    nt = (((1,), (1,)), ((), ()))
    rowi = lax.broadcasted_iota(jnp.int32, (SUBLANES, tm), 0)
    zeros56 = jnp.zeros((QK_DIM - SUBLANES, tm), F32)

    yt = lax.dot_general(wt_ref[0:512, :], u, nt, preferred_element_type=F32)
    for h in range(N_DIFF_HEADS):
        tiles = [yt[h * LANES + r:h * LANES + r + SUBLANES] for r in range(0, LANES, SUBLANES)]
        for base in (0, QK_DIM // SUBLANES):
            x1, x2 = tiles[base], tiles[base + 1]
            tiles[base] = x1 * ct - x2 * st
            tiles[base + 1] = x2 * ct + x1 * st
        qt_ref[h] = (jnp.concatenate(tiles, axis=0) * scale).astype(BF16)

    yt = lax.dot_general(wt_ref[512:1024, :], u, nt, preferred_element_type=F32)
    for h in range(N_FOX_HEADS):
        p, e = divmod(h, 2)
        src = (QK_DIM if e == 0 else 0) + SUBLANES * p
        hi, mid, lo = _bf16_pieces(cst[src:src + SUBLANES])
        aug8 = jnp.where(rowi < 3, 1.0, jnp.where(rowi == 3, hi, jnp.where(
            rowi == 4, mid, jnp.where(rowi == 5, lo, 0.0))))
        qh = yt[h * QK_DIM:(h + 1) * QK_DIM] * scale
        blocks = [qh, aug8, zeros56] if e == 0 else [aug8, zeros56, qh]
        qt_ref[N_DIFF_HEADS + h] = jnp.concatenate(blocks, axis=0).astype(BF16)

    tk = vt_ref.shape[-1]
    for c in range(2):
        yt = lax.dot_general(wt_ref[1024 + 512 * c:1536 + 512 * c, :], u, nt,
                             preferred_element_type=F32)
        for j in range(4):
            blk = jnp.concatenate([yt[j * LANES:(j + 1) * LANES], jnp.ones((ONES_ROWS, tm), F32)],
                                  axis=0).astype(BF16)
            for t in range(tm // tk):
                vt_ref[4 * c + j, t] = blk[:, t * tk:(t + 1) * tk]


def _inproj(x2, g, w_nat, w_tr, w_gate, b_gate, rope_nat, rope_tr, *, batch, seq, tm, tk):
    rows, d = x2.shape
    tiles_per_batch = seq // tm
    const = lambda i: (0, 0)
    tile = lambda i: (i % tiles_per_batch, 0)
    tile_t = lambda i: (0, i % tiles_per_batch)
    bt = lambda i: (i // tiles_per_batch, 0, i % tiles_per_batch, 0)
    bt5 = lambda i: (i // tiles_per_batch, 0, i % tiles_per_batch, 0, 0)
    return pl.pallas_call(
        functools.partial(_inproj_kernel, tiles_per_batch=tiles_per_batch, tm=tm),
        grid=(rows // tm,),
        in_specs=[
            pl.BlockSpec((tm, d), lambda i: (i, 0)),
            pl.BlockSpec((1, d), const),
            pl.BlockSpec(w_nat.shape, const),
            pl.BlockSpec(w_tr.shape, const),
            pl.BlockSpec(w_gate.shape, const),
            pl.BlockSpec((1, LANES), const),
            pl.BlockSpec((tm, LANES), tile),
            pl.BlockSpec((tm, LANES), tile),
            pl.BlockSpec((tm, LANES), tile),
            pl.BlockSpec((SUBLANES, tm), tile_t),
            pl.BlockSpec((SUBLANES, tm), tile_t),
        ],
        out_specs=[
            pl.BlockSpec((None, QT_BLOCKS, None, LANES, tm), bt5),
            pl.BlockSpec((None, K_BLOCKS, tm, LANES), bt),
            pl.BlockSpec((None, VT_BLOCKS, tm // tk, VT_ROWS, tk), bt5),
        ],
        out_shape=[
            jax.ShapeDtypeStruct((batch, QT_BLOCKS, seq // tm, LANES, tm), BF16),
            jax.ShapeDtypeStruct((batch, K_BLOCKS, seq, LANES), BF16),
            jax.ShapeDtypeStruct((batch, VT_BLOCKS, seq // tk, VT_ROWS, tk), BF16),
        ],
        scratch_shapes=[pltpu.VMEM((SUBLANES, LANES), F32)],
        compiler_params=pltpu.CompilerParams(
            dimension_semantics=("arbitrary",), vmem_limit_bytes=VMEM_LIMIT),
        name="inproj",
    )(x2, g, w_nat, w_tr, w_gate, b_gate, *rope_nat, *rope_tr)


def _attn_kernel(*refs, kind, tq, tk, seq):
    if kind == "diff":
        lam_ref, g_ref, qt_ref, k_ref, vt_ref, o_ref = refs[:6]
        k_refs = (k_ref,)
    else:
        g_ref, qt0_ref, qt1_ref, k0_ref, k1_ref, vt_ref, o_ref = refs[:7]
        k_refs = (k0_ref, k1_ref)
    acc_scs, qs_scs, s_scs = refs[-6:-4], refs[-4:-2], refs[-2:]
    n = 2 * tq
    assert tq == tk
    rowq = lax.broadcasted_iota(jnp.int32, (LANES, tq), 0)

    sw = MXU_COLS
    n_strips = n // sw
    strips_per_group = tq // sw

    def strip_keys(st, masked):
        return (st % strips_per_group + 1) * sw if masked else tk

    def scores(t, qs_sc, s_ref, masked):
        maxes = []
        if masked:
            for st in range(n_strips):
                nk = strip_keys(st, True)
                kr = k_refs[st * len(k_refs) // n_strips]
                s = jnp.dot(kr[t * tk:t * tk + nk, :], qs_sc[:, st * sw:(st + 1) * sw],
                            preferred_element_type=F32)
                key_l = lax.broadcasted_iota(jnp.int32, s.shape, 0)
                qry_l = lax.broadcasted_iota(jnp.int32, s.shape, 1) + (st % strips_per_group) * sw
                s = jnp.where(key_l <= qry_l, s, NEG)
                s_ref[:nk, st * sw:(st + 1) * sw] = s
                maxes.append(jnp.max(s, axis=0, keepdims=True))
        else:
            w = n // len(k_refs)
            for j, kr in enumerate(k_refs):
                s = jnp.dot(kr[t * tk:(t + 1) * tk, :], qs_sc[:, j * w:(j + 1) * w],
                            preferred_element_type=F32)
                s_ref[:, j * w:(j + 1) * w] = s
                maxes.append(jnp.max(s, axis=0, keepdims=True))
        return maxes[0] if len(maxes) == 1 else jnp.concatenate(maxes, axis=1)

    def absorb(s_ref, m_cur, t, ml, acc_sc, masked):
        m, l = ml
        m_next = jnp.maximum(m, m_cur)
        alpha = jnp.exp2(m - m_next)
        sums = []
        width = sw if masked else n
        for st in range(n // width):
            nk = strip_keys(st, masked)
            sl = slice(st * width, (st + 1) * width)
            p = jnp.exp2(s_ref[:nk, sl] - m_next[:, sl]).astype(BF16)
            pv = jnp.dot(vt_ref[t, :, :nk], p, preferred_element_type=F32)
            acc_sc[:, sl] = acc_sc[:, sl] * alpha[:, sl] + pv[:LANES]
            sums.append(pv[LANES:LANES + 1])
        return m_next, alpha * l + jnp.concatenate(sums, axis=1)

    def fill_qs(qi, qs_sc):
        if kind == "diff":
            qt = qt_ref[qi]
            zero = jnp.zeros_like(qt)
            qs_sc[:, :tq] = jnp.where(rowq < QK_DIM, qt, zero)
            qs_sc[:, tq:] = jnp.where(rowq < QK_DIM, zero, qt)
        else:
            qs_sc[:, :tq] = qt0_ref[qi]
            qs_sc[:, tq:] = qt1_ref[qi]

    def finalize(qi, acc_sc, ml):
        m, l = ml
        acc = acc_sc[...]
        o0 = acc[:, :tq] / l[:, :tq]
        o1 = acc[:, tq:] / l[:, tq:]
        g = g_ref[...]
        if kind == "diff":
            lam = (jnp.exp(jnp.sum(lam_ref[0:1, :] * lam_ref[1:2, :], axis=1, keepdims=True))
                   - jnp.exp(jnp.sum(lam_ref[2:3, :] * lam_ref[3:4, :], axis=1, keepdims=True))
                   + LAMBDA_INIT)
            o = o0 - lam * o1
            ms = jnp.mean(o * o, axis=0, keepdims=True)
            o = o * lax.rsqrt(ms + SUBLN_EPS) * g * (1.0 - LAMBDA_INIT)
        else:
            halves = []
            for oh in (o0[:QK_DIM], o1[QK_DIM:]):
                ms = jnp.mean(oh * oh, axis=0, keepdims=True)
                halves.append(oh * lax.rsqrt(ms + EPS))
            o = jnp.concatenate(halves, axis=0) * g
        o_ref[qi * tq:(qi + 1) * tq, :] = o.T.astype(o_ref.dtype)

    steps = [(qi, t, t == qi) for qi in range(seq // tq) for t in [qi] + list(range(qi))]
    fill_qs(0, qs_scs[0])
    mc = scores(0, qs_scs[0], s_scs[0], True)

    def run_steps(g0, g1, state):
        mc, m, l = state
        ml = (m, l)
        for g in range(g0, g1):
            qi, t, _ = steps[g]
            acc_sc = acc_scs[qi % 2]
            if t == qi:
                acc_sc[...] = jnp.zeros(acc_sc.shape, F32)
                ml = (jnp.full((1, n), NEG, F32), jnp.zeros((1, n), F32))
            if g + 1 < len(steps):
                qi2, t2, masked2 = steps[g + 1]
                if qi2 != qi:
                    fill_qs(qi2, qs_scs[qi2 % 2])
                mc_next = scores(t2, qs_scs[qi2 % 2], s_scs[(g + 1) % len(s_scs)], masked2)
            else:
                mc_next = mc
            ml = absorb(s_scs[g % len(s_scs)], mc, t, ml, acc_sc, t == qi)
            mc = mc_next
            if g + 1 == len(steps) or steps[g + 1][0] != qi:
                finalize(qi, acc_sc, ml)
        return mc, ml[0], ml[1]

    once = jnp.minimum(pl.program_id(0) + 1, 1)
    state = (mc, jnp.full((1, n), NEG, F32), jnp.zeros((1, n), F32))
    for g0 in range(0, len(steps), STEPS_PER_REGION):
        g1 = min(g0 + STEPS_PER_REGION, len(steps))
        state = lax.fori_loop(0, once, lambda _, st, g0=g0, g1=g1: run_steps(g0, g1, st), state)


def _attention(kind, qt, kn, vt, extra, *, tq, tk):
    batch, _, nq, _, _ = qt.shape
    seq = kn.shape[2]
    nk = vt.shape[2]
    const = lambda b, h: (0, 0)
    qspec = lambda f: pl.BlockSpec((None, None, nq, LANES, tq), lambda b, h: (b, f(h), 0, 0, 0))
    kspec = lambda f: pl.BlockSpec((None, None, seq, LANES), lambda b, h: (b, f(h), 0, 0))
    if kind == "diff":
        lam4, g = extra
        in_specs = [pl.BlockSpec(lam4.shape, const), pl.BlockSpec(g.shape, const),
                    qspec(lambda h: h), kspec(lambda h: h)]
        args = [lam4, g, qt, kn]
        v_base = 0
    else:
        (g,) = extra
        in_specs = [pl.BlockSpec(g.shape, const),
                    qspec(lambda h: N_DIFF_HEADS + 2 * h), qspec(lambda h: N_DIFF_HEADS + 2 * h + 1),
                    kspec(lambda h: N_DIFF_HEADS + 2 * h), kspec(lambda h: N_DIFF_HEADS + 2 * h + 1)]
        args = [g, qt, qt, kn, kn]
        v_base = N_DIFF_HEADS
    in_specs.append(pl.BlockSpec((None, None, nk, VT_ROWS, tk), lambda b, h: (b, v_base + h, 0, 0, 0)))
    args.append(vt)
    n_groups = 4
    return pl.pallas_call(
        functools.partial(_attn_kernel, kind=kind, tq=tq, tk=tk, seq=seq),
        grid=(batch, n_groups),
        in_specs=in_specs,
        out_specs=pl.BlockSpec((None, seq, LANES), lambda b, h: (b, 0, h)),
        out_shape=jax.ShapeDtypeStruct((batch, seq, n_groups * LANES), BF16),
        scratch_shapes=([pltpu.VMEM((LANES, 2 * tq), F32)] * 2
                        + [pltpu.VMEM((LANES, 2 * tq), BF16)] * 2
                        + [pltpu.VMEM((tk, 2 * tq), F32)] * 2),
        compiler_params=pltpu.CompilerParams(
            dimension_semantics=("arbitrary", "arbitrary"), vmem_limit_bytes=VMEM_LIMIT),
        name=kind + "_attn",
    )(*args)


def _memkv_kernel(mem_ref, g_ref, w_ref, kv_ref):
    u = _rms(mem_ref[...], g_ref[...], EPS).astype(BF16)
    kv_ref[...] = jnp.dot(u, w_ref[...], preferred_element_type=F32).astype(BF16)


def _memkv(mem2, g, w, *, tm):
    rows, d = mem2.shape
    n = w.shape[1]
    return pl.pallas_call(
        _memkv_kernel,
        grid=(rows // tm,),
        in_specs=[pl.BlockSpec((tm, d), lambda i: (i, 0)),
                  pl.BlockSpec((1, d), lambda i: (0, 0)),
                  pl.BlockSpec(w.shape, lambda i: (0, 0))],
        out_specs=pl.BlockSpec((tm, n), lambda i: (i, 0)),
        out_shape=jax.ShapeDtypeStruct((rows, n), BF16),
        compiler_params=pltpu.CompilerParams(
            dimension_semantics=("arbitrary",), vmem_limit_bytes=VMEM_LIMIT),
        name="memkv",
    )(mem2, g, w)


def _tail_kernel(x_ref, d_ref, f_ref, wo_ref, g_ref, wq_ref, kv_ref, wco_ref,
                 gm_ref, wu_ref, wd_ref, gf_ref, o_ref, *, fc):
    half = d_ref.shape[1]
    h1 = (x_ref[...]
          + jnp.dot(d_ref[...], wo_ref[:half, :], preferred_element_type=F32)
          + jnp.dot(f_ref[...], wo_ref[half:, :], preferred_element_type=F32))
    uq = _rms(h1, g_ref[...], EPS).astype(BF16)
    dm = h1.shape[1]
    hd = dm // N_CROSS_HEADS
    cq = (jnp.dot(uq, wq_ref[...], preferred_element_type=F32) * (hd ** -0.5)).astype(BF16)
    outs = []
    for hh in range(N_CROSS_HEADS):
        qh = cq[:, hh * hd:(hh + 1) * hd]
        kh = kv_ref[:, hh * hd:(hh + 1) * hd]
        vh = kv_ref[:, dm + hh * hd:dm + (hh + 1) * hd]
        s = lax.dot_general(qh, kh, (((1,), (1,)), ((), ())), preferred_element_type=F32)
        p = jnp.exp(s - jnp.max(s, axis=1, keepdims=True))
        l = jnp.sum(p, axis=1, keepdims=True)
        oh = jnp.dot(p.astype(BF16), vh, preferred_element_type=F32) / l
        outs.append(oh.astype(BF16))
    co = jnp.concatenate(outs, axis=1)
    h2 = h1 + jnp.dot(co, wco_ref[...], preferred_element_type=F32)

    u = _rms(h2, gm_ref[...], EPS).astype(BF16)
    acc = h2
    for c in range(wu_ref.shape[1] // fc):
        z = jnp.dot(u, wu_ref[:, c * fc:(c + 1) * fc], preferred_element_type=F32)
        a = jnp.square(jnp.maximum(z, 0.0)).astype(BF16)
        acc = acc + jnp.dot(a, wd_ref[c * fc:(c + 1) * fc, :], preferred_element_type=F32)
    o_ref[...] = _rms(acc, gf_ref[...], EPS)


def _tail(x2, d2, f2, w_out, g_cross, w_cq, kv, w_co, g_mlp, w_up, w_down, g_final, *, seq, tm, fc):
    rows, dm = x2.shape
    tiles_per_batch = seq // tm
    n_mem = kv.shape[1]
    const = lambda i: (0, 0)
    resident = lambda w: pl.BlockSpec(w.shape, const, pipeline_mode=pl.Buffered(1))
    return pl.pallas_call(
        functools.partial(_tail_kernel, fc=fc),
        grid=(rows // tm,),
        in_specs=[
            pl.BlockSpec((tm, dm), lambda i: (i, 0)),
            pl.BlockSpec((tm, d2.shape[1]), lambda i: (i, 0)),
            pl.BlockSpec((tm, f2.shape[1]), lambda i: (i, 0)),
            resident(w_out),
            pl.BlockSpec((1, dm), const),
            resident(w_cq),
            pl.BlockSpec((None, n_mem, kv.shape[2]), lambda i: (i // tiles_per_batch, 0, 0)),
            resident(w_co),
            pl.BlockSpec((1, dm), const),
            resident(w_up),
            resident(w_down),
            pl.BlockSpec((1, dm), const),
        ],
        out_specs=pl.BlockSpec((tm, dm), lambda i: (i, 0)),
        out_shape=jax.ShapeDtypeStruct((rows, dm), F32),
        compiler_params=pltpu.CompilerParams(
            dimension_semantics=("arbitrary",), vmem_limit_bytes=VMEM_LIMIT),
        name="tail",
    )(x2, d2, f2, w_out, g_cross, w_cq, kv, w_co, g_mlp, w_up, w_down, g_final)


def _rope_tables(seq):
    pos = jnp.arange(seq, dtype=F32)
    inv_freq = ROPE_THETA ** (-jnp.arange(0, ROT_DIM, 2, dtype=F32) / ROT_DIM)
    ang = pos[:, None] * inv_freq[None, :]
    cos, sin = jnp.cos(ang), jnp.sin(ang)
    lane = jnp.arange(LANES)
    l64 = lane % QK_DIM
    fidx = l64 % (ROT_DIM // 2)
    in_rot = (l64 < ROT_DIM)[None, :]
    first = (l64 < ROT_DIM // 2)[None, :]
    rc = jnp.where(in_rot, cos[:, fidx], 1.0)
    ra = jnp.where(in_rot & ~first, sin[:, fidx], 0.0)
    rb = jnp.where(first, -sin[:, fidx], 0.0)
    return (rc.astype(F32), ra.astype(F32), rb.astype(F32)), (cos.T.astype(F32), sin.T.astype(F32))


def _gate_lane_layout(cols):
    odd = jnp.repeat(cols[..., 1::2], SUBLANES, axis=-1)
    even = jnp.repeat(cols[..., 0::2], SUBLANES, axis=-1)
    pad = jnp.zeros(cols.shape[:-1] + (QK_DIM - odd.shape[-1],), cols.dtype)
    return jnp.concatenate([odd, pad, even, pad], axis=-1)


def kernel(x, mem, norm_mix_g, w_in, b_forget, lam_q1, lam_k1, lam_q2, lam_k2,
           diff_subln_g, fox_out_g, w_out, norm_cross_g, norm_mem_g, w_cq, w_ckv, w_co,
           norm_mlp_g, w_up, w_down, norm_final_g):
    batch, seq, dm = x.shape
    n_mem = mem.shape[1]
    assert norm_mix_g.shape[0] == 1, "single-layer block"
    assert w_in.shape[2] == 6 * 512 + N_FOX_HEADS

    x2 = x.reshape(batch * seq, dm)
    w = w_in[0]
    dq, dk, dv, fq, fk, fv = (w[:, c * 512:(c + 1) * 512] for c in range(6))
    w_nat = jnp.concatenate([dk, fk], axis=1).astype(BF16)
    w_tr = jnp.concatenate([dq, fq, dv, fv], axis=1).T.astype(BF16)
    w_gate = _gate_lane_layout(w[:, 6 * 512:]).astype(BF16)
    b_gate = _gate_lane_layout(b_forget[0]).reshape(1, LANES)
    rope_nat, rope_tr = _rope_tables(seq)

    qt, kn, vt = _inproj(x2, norm_mix_g[0].reshape(1, dm), w_nat, w_tr, w_gate, b_gate,
                         rope_nat, rope_tr, batch=batch, seq=seq, tm=ROW_TILE, tk=ATTN_TK)

    lam4 = jnp.stack([lam_q1[0], lam_k1[0], lam_q2[0], lam_k2[0]], axis=0)
    d_out = _attention("diff", qt, kn, vt, (lam4, diff_subln_g[0].reshape(LANES, 1)),
                       tq=ATTN_TQ, tk=ATTN_TK)
    f_out = _attention("fox", qt, kn, vt, (jnp.tile(fox_out_g[0], 2).reshape(LANES, 1),),
                       tq=ATTN_TQ, tk=ATTN_TK)

    kv = _memkv(mem.reshape(batch * n_mem, dm), norm_mem_g[0].reshape(1, dm),
                w_ckv[0].astype(BF16), tm=ROW_TILE)
    out = _tail(x2, d_out.reshape(batch * seq, -1), f_out.reshape(batch * seq, -1),
                w_out[0].astype(BF16), norm_cross_g[0].reshape(1, dm), w_cq[0].astype(BF16),
                kv.reshape(batch, n_mem, -1), w_co[0].astype(BF16),
                norm_mlp_g[0].reshape(1, dm), w_up[0].astype(BF16), w_down[0].astype(BF16),
                norm_final_g.reshape(1, dm), seq=seq, tm=ROW_TILE, fc=MLP_FF_CHUNK)
    return out.reshape(batch, seq, dm)
```

```python
import functools

import jax
import jax.numpy as jnp
from jax import lax
from jax.experimental import pallas as pl
from jax.experimental.pallas import tpu as pltpu

F32 = jnp.float32
BF16 = jnp.bfloat16

LANES = 128
SUBLANES = 8
MXU_COLS = 256
QK_DIM = 64
ROT_DIM = 16
ROPE_THETA = 500000.0
N_DIFF_HEADS = 4
N_FOX_HEADS = 8
N_FOX_PAIRS = N_FOX_HEADS // 2
N_CROSS_HEADS = 4
EPS = 1e-6
SUBLN_EPS = 1e-5
LAMBDA_INIT = 0.8 - 0.6 * 1.0
NEG = -1e30
LOG2E = 1.4426950408889634

VMEM_LIMIT = 56 * 1024 * 1024

ROW_TILE = 512
ATTN_TQ = 512
ATTN_TK = 512
MLP_FF_CHUNK = 1024
STEPS_PER_REGION = 12

QT_BLOCKS = N_DIFF_HEADS + N_FOX_HEADS
K_BLOCKS = N_DIFF_HEADS + N_FOX_HEADS
VT_BLOCKS = N_DIFF_HEADS + N_FOX_PAIRS
ONES_ROWS = 16
VT_ROWS = LANES + ONES_ROWS


def _rms(x, g, eps):
    ms = jnp.mean(x * x, axis=-1, keepdims=True)
    return x * lax.rsqrt(ms + eps) * g


def _bf16_pieces(c):
    hi = c.astype(BF16).astype(F32)
    r1 = c - hi
    mid = r1.astype(BF16).astype(F32)
    lo = (r1 - mid).astype(BF16).astype(F32)
    return hi, mid, lo


def _inproj_kernel(x_ref, g_ref, wn_ref, wt_ref, bg_ref, rc_ref, ra_ref, rb_ref,
                   ct_ref, st_ref, qt_ref, k_ref, vt_ref, carry_ref, *, tiles_per_batch, tm):
    i = pl.program_id(0)
    u = _rms(x_ref[...], g_ref[...], EPS).astype(BF16)
    scale = QK_DIM ** -0.5 * LOG2E

    yn = jnp.dot(u, wn_ref[...], preferred_element_type=F32)

    gate = yn[:, 2 * 512:] + bg_ref[...]
    logf = jnp.minimum(gate, 0.0) - jnp.log1p(jnp.exp(-jnp.abs(gate)))

    @pl.when(i % tiles_per_batch == 0)
    def _():
        carry_ref[...] = jnp.zeros_like(carry_ref)

    rowt = lax.broadcasted_iota(jnp.int32, (tm, LANES), 0)
    cs = logf
    shift = 1
    while shift < tm:
        cs = cs + jnp.where(rowt >= shift, pltpu.roll(cs, shift, 0), 0.0)
        shift *= 2
    cs = cs + carry_ref[0:1, :]
    carry_ref[...] = jnp.broadcast_to(cs[tm - 1:tm, :], carry_ref.shape)
    cs = cs * LOG2E
    cst = cs.T

    rc, ra, rb = rc_ref[...], ra_ref[...], rb_ref[...]
    lane = lax.broadcasted_iota(jnp.int32, (tm, LANES), 1)
    low = lane < QK_DIM
    l64 = lane % QK_DIM
    for h in range(N_DIFF_HEADS):
        t = yn[:, h * LANES:(h + 1) * LANES]
        t = t * rc + pltpu.roll(t, 8, 1) * ra + pltpu.roll(t, LANES - 8, 1) * rb
        k_ref[h] = t.astype(BF16)
    for p in range(N_FOX_PAIRS):
        yp = yn[:, (N_DIFF_HEADS + p) * LANES:(N_DIFF_HEADS + p + 1) * LANES]
        rolled = cs if p == 0 else pltpu.roll(cs, LANES - 8 * p, 1)
        hi, mid, lo = _bf16_pieces(rolled)
        augk = jnp.where(l64 == 0, -hi, jnp.where(l64 == 1, -mid, jnp.where(
            l64 == 2, -lo, jnp.where(l64 < 6, 1.0, 0.0))))
        k_ref[N_DIFF_HEADS + 2 * p] = jnp.where(low, yp, augk).astype(BF16)
        k_ref[N_DIFF_HEADS + 2 * p + 1] = jnp.where(low, augk, yp).astype(BF16)

    ct, st = ct_ref[...], st_ref[...]
    nt = (((1,), (1,)), ((), ()))
    rowi = lax.broadcasted_iota(jnp.int32, (SUBLANES, tm), 0)
    zeros56 = jnp.zeros((QK_DIM - SUBLANES, tm), F32)

    yt = lax.dot_general(wt_ref[0:512, :], u, nt, preferred_element_type=F32)
    for h in range(N_DIFF_HEADS):
        tiles = [yt[h * LANES + r:h * LANES + r + SUBLANES] for r in range(0, LANES, SUBLANES)]
        for base in (0, QK_DIM // SUBLANES):
            x1, x2 = tiles[base], tiles[base + 1]
            tiles[base] = x1 * ct - x2 * st
            tiles[base + 1] = x2 * ct + x1 * st
        qt_ref[h] = (jnp.concatenate(tiles, axis=0) * scale).astype(BF16)

    yt = lax.dot_general(wt_ref[512:1024, :], u, nt, preferred_element_type=F32)
    for h in range(N_FOX_HEADS):
        p, e = divmod(h, 2)
        src = (QK_DIM if e == 0 else 0) + SUBLANES * p
        hi, mid, lo = _bf16_pieces(cst[src:src + SUBLANES])
        aug8 = jnp.where(rowi < 3, 1.0, jnp.where(rowi == 3, hi, jnp.where(
            rowi == 4, mid, jnp.where(rowi == 5, lo, 0.0))))
        qh = yt[h * QK_DIM:(h + 1) * QK_DIM] * scale
        blocks = [qh, aug8, zeros56] if e == 0 else [aug8, zeros56, qh]
        qt_ref[N_DIFF_HEADS + h] = jnp.concatenate(blocks, axis=0).astype(BF16)

    tk = vt_ref.shape[-1]
    for c in range(2):
        yt = lax.dot_general(wt_ref[1024 + 512 * c:1536 + 512 * c, :], u, nt,
                             preferred_element_type=F32)
        for j in range(4):
            blk = jnp.concatenate([yt[j * LANES:(j + 1) * LANES], jnp.ones((ONES_ROWS, tm), F32)],
                                  axis=0).astype(BF16)
            for t in range(tm // tk):
                vt_ref[4 * c + j, t] = blk[:, t * tk:(t + 1) * tk]


def _inproj(x2, g, w_nat, w_tr, b_gate, rope_nat, rope_tr, *, batch, seq, tm, tk):
    rows, d = x2.shape
    tiles_per_batch = seq // tm
    const = lambda i: (0, 0)
    tile = lambda i: (i % tiles_per_batch, 0)
    tile_t = lambda i: (0, i % tiles_per_batch)
    bt = lambda i: (i // tiles_per_batch, 0, i % tiles_per_batch, 0)
    bt5 = lambda i: (i // tiles_per_batch, 0, i % tiles_per_batch, 0, 0)
    return pl.pallas_call(
        functools.partial(_inproj_kernel, tiles_per_batch=tiles_per_batch, tm=tm),
        grid=(rows // tm,),
        in_specs=[
            pl.BlockSpec((tm, d), lambda i: (i, 0)),
            pl.BlockSpec((1, d), const),
            pl.BlockSpec(w_nat.shape, const),
            pl.BlockSpec(w_tr.shape, const),
            pl.BlockSpec((1, LANES), const),
            pl.BlockSpec((tm, LANES), tile),
            pl.BlockSpec((tm, LANES), tile),
            pl.BlockSpec((tm, LANES), tile),
            pl.BlockSpec((SUBLANES, tm), tile_t),
            pl.BlockSpec((SUBLANES, tm), tile_t),
        ],
        out_specs=[
            pl.BlockSpec((None, QT_BLOCKS, None, LANES, tm), bt5),
            pl.BlockSpec((None, K_BLOCKS, tm, LANES), bt),
            pl.BlockSpec((None, VT_BLOCKS, tm // tk, VT_ROWS, tk), bt5),
        ],
        out_shape=[
            jax.ShapeDtypeStruct((batch, QT_BLOCKS, seq // tm, LANES, tm), BF16),
            jax.ShapeDtypeStruct((batch, K_BLOCKS, seq, LANES), BF16),
            jax.ShapeDtypeStruct((batch, VT_BLOCKS, seq // tk, VT_ROWS, tk), BF16),
        ],
        scratch_shapes=[pltpu.VMEM((SUBLANES, LANES), F32)],
        compiler_params=pltpu.CompilerParams(
            dimension_semantics=("arbitrary",), vmem_limit_bytes=VMEM_LIMIT),
        name="inproj",
    )(x2, g, w_nat, w_tr, b_gate, *rope_nat, *rope_tr)


def _attn_kernel(*refs, kind, tq, tk, seq):
    if kind == "diff":
        lam_ref, g_ref, qt_ref, k_ref, vt_ref, o_ref = refs[:6]
        k_refs = (k_ref,)
    else:
        g_ref, qt0_ref, qt1_ref, k0_ref, k1_ref, vt_ref, o_ref = refs[:7]
        k_refs = (k0_ref, k1_ref)
    acc_scs, qs_scs, s_scs = refs[-6:-4], refs[-4:-2], refs[-2:]
    n = 2 * tq
    assert tq == tk
    rowq = lax.broadcasted_iota(jnp.int32, (LANES, tq), 0)

    sw = MXU_COLS
    n_strips = n // sw
    strips_per_group = tq // sw

    def strip_keys(st, masked):
        return (st % strips_per_group + 1) * sw if masked else tk

    def scores(t, qs_sc, s_ref, masked):
        maxes = []
        if masked:
            for st in range(n_strips):
                nk = strip_keys(st, True)
                kr = k_refs[st * len(k_refs) // n_strips]
                s = jnp.dot(kr[t * tk:t * tk + nk, :], qs_sc[:, st * sw:(st + 1) * sw],
                            preferred_element_type=F32)
                key_l = lax.broadcasted_iota(jnp.int32, s.shape, 0)
                qry_l = lax.broadcasted_iota(jnp.int32, s.shape, 1) + (st % strips_per_group) * sw
                s = jnp.where(key_l <= qry_l, s, NEG)
                s_ref[:nk, st * sw:(st + 1) * sw] = s
                maxes.append(jnp.max(s, axis=0, keepdims=True))
        else:
            w = n // len(k_refs)
            for j, kr in enumerate(k_refs):
                s = jnp.dot(kr[t * tk:(t + 1) * tk, :], qs_sc[:, j * w:(j + 1) * w],
                            preferred_element_type=F32)
                s_ref[:, j * w:(j + 1) * w] = s
                maxes.append(jnp.max(s, axis=0, keepdims=True))
        return maxes[0] if len(maxes) == 1 else jnp.concatenate(maxes, axis=1)

    def absorb(s_ref, m_cur, t, ml, acc_sc, masked):
        m, l = ml
        m_next = jnp.maximum(m, m_cur)
        alpha = jnp.exp2(m - m_next)
        sums = []
        width = sw if masked else n
        for st in range(n // width):
            nk = strip_keys(st, masked)
            sl = slice(st * width, (st + 1) * width)
            p = jnp.exp2(s_ref[:nk, sl] - m_next[:, sl]).astype(BF16)
            pv = jnp.dot(vt_ref[t, :, :nk], p, preferred_element_type=F32)
            acc_sc[:, sl] = acc_sc[:, sl] * alpha[:, sl] + pv[:LANES]
            sums.append(pv[LANES:LANES + 1])
        return m_next, alpha * l + jnp.concatenate(sums, axis=1)

    def fill_qs(qi, qs_sc):
        if kind == "diff":
            qt = qt_ref[qi]
            zero = jnp.zeros_like(qt)
            qs_sc[:, :tq] = jnp.where(rowq < QK_DIM, qt, zero)
            qs_sc[:, tq:] = jnp.where(rowq < QK_DIM, zero, qt)
        else:
            qs_sc[:, :tq] = qt0_ref[qi]
            qs_sc[:, tq:] = qt1_ref[qi]

    def finalize(qi, acc_sc, ml):
        m, l = ml
        acc = acc_sc[...]
        o0 = acc[:, :tq] / l[:, :tq]
        o1 = acc[:, tq:] / l[:, tq:]
        g = g_ref[...]
        if kind == "diff":
            lam = (jnp.exp(jnp.sum(lam_ref[0:1, :] * lam_ref[1:2, :], axis=1, keepdims=True))
                   - jnp.exp(jnp.sum(lam_ref[2:3, :] * lam_ref[3:4, :], axis=1, keepdims=True))
                   + LAMBDA_INIT)
            o = o0 - lam * o1
            ms = jnp.mean(o * o, axis=0, keepdims=True)
            o = o * lax.rsqrt(ms + SUBLN_EPS) * g * (1.0 - LAMBDA_INIT)
        else:
            halves = []
            for oh in (o0[:QK_DIM], o1[QK_DIM:]):
                ms = jnp.mean(oh * oh, axis=0, keepdims=True)
                halves.append(oh * lax.rsqrt(ms + EPS))
            o = jnp.concatenate(halves, axis=0) * g
        o_ref[qi * tq:(qi + 1) * tq, :] = o.T.astype(o_ref.dtype)

    steps = [(qi, t, t == qi) for qi in range(seq // tq) for t in [qi] + list(range(qi))]
    fill_qs(0, qs_scs[0])
    mc = scores(0, qs_scs[0], s_scs[0], True)

    def run_steps(g0, g1, state):
        mc, m, l = state
        ml = (m, l)
        for g in range(g0, g1):
            qi, t, _ = steps[g]
            acc_sc = acc_scs[qi % 2]
            if t == qi:
                acc_sc[...] = jnp.zeros(acc_sc.shape, F32)
                ml = (jnp.full((1, n), NEG, F32), jnp.zeros((1, n), F32))
            if g + 1 < len(steps):
                qi2, t2, masked2 = steps[g + 1]
                if qi2 != qi:
                    fill_qs(qi2, qs_scs[qi2 % 2])
                mc_next = scores(t2, qs_scs[qi2 % 2], s_scs[(g + 1) % len(s_scs)], masked2)
            else:
                mc_next = mc
            ml = absorb(s_scs[g % len(s_scs)], mc, t, ml, acc_sc, t == qi)
            mc = mc_next
            if g + 1 == len(steps) or steps[g + 1][0] != qi:
                finalize(qi, acc_sc, ml)
        return mc, ml[0], ml[1]

    once = jnp.minimum(pl.program_id(0) + 1, 1)
    state = (mc, jnp.full((1, n), NEG, F32), jnp.zeros((1, n), F32))
    for g0 in range(0, len(steps), STEPS_PER_REGION):
        g1 = min(g0 + STEPS_PER_REGION, len(steps))
        state = lax.fori_loop(0, once, lambda _, st, g0=g0, g1=g1: run_steps(g0, g1, st), state)


def _attention(kind, qt, kn, vt, extra, *, tq, tk):
    batch, _, nq, _, _ = qt.shape
    seq = kn.shape[2]
    nk = vt.shape[2]
    const = lambda b, h: (0, 0)
    qspec = lambda f: pl.BlockSpec((None, None, nq, LANES, tq), lambda b, h: (b, f(h), 0, 0, 0))
    kspec = lambda f: pl.BlockSpec((None, None, seq, LANES), lambda b, h: (b, f(h), 0, 0))
    if kind == "diff":
        lam4, g = extra
        in_specs = [pl.BlockSpec(lam4.shape, const), pl.BlockSpec(g.shape, const),
                    qspec(lambda h: h), kspec(lambda h: h)]
        args = [lam4, g, qt, kn]
        v_base = 0
    else:
        (g,) = extra
        in_specs = [pl.BlockSpec(g.shape, const),
                    qspec(lambda h: N_DIFF_HEADS + 2 * h), qspec(lambda h: N_DIFF_HEADS + 2 * h + 1),
                    kspec(lambda h: N_DIFF_HEADS + 2 * h), kspec(lambda h: N_DIFF_HEADS + 2 * h + 1)]
        args = [g, qt, qt, kn, kn]
        v_base = N_DIFF_HEADS
    in_specs.append(pl.BlockSpec((None, None, nk, VT_ROWS, tk), lambda b, h: (b, v_base + h, 0, 0, 0)))
    args.append(vt)
    n_groups = 4
    return pl.pallas_call(
        functools.partial(_attn_kernel, kind=kind, tq=tq, tk=tk, seq=seq),
        grid=(batch, n_groups),
        in_specs=in_specs,
        out_specs=pl.BlockSpec((None, seq, LANES), lambda b, h: (b, 0, h)),
        out_shape=jax.ShapeDtypeStruct((batch, seq, n_groups * LANES), BF16),
        scratch_shapes=([pltpu.VMEM((LANES, 2 * tq), F32)] * 2
                        + [pltpu.VMEM((LANES, 2 * tq), BF16)] * 2
                        + [pltpu.VMEM((tk, 2 * tq), F32)] * 2),
        compiler_params=pltpu.CompilerParams(
            dimension_semantics=("arbitrary", "arbitrary"), vmem_limit_bytes=VMEM_LIMIT),
        name=kind + "_attn",
    )(*args)


def _memkv_kernel(mem_ref, g_ref, w_ref, kv_ref):
    u = _rms(mem_ref[...], g_ref[...], EPS).astype(BF16)
    kv_ref[...] = jnp.dot(u, w_ref[...], preferred_element_type=F32).astype(BF16)


def _memkv(mem2, g, w, *, tm):
    rows, d = mem2.shape
    n = w.shape[1]
    return pl.pallas_call(
        _memkv_kernel,
        grid=(rows // tm,),
        in_specs=[pl.BlockSpec((tm, d), lambda i: (i, 0)),
                  pl.BlockSpec((1, d), lambda i: (0, 0)),
                  pl.BlockSpec(w.shape, lambda i: (0, 0))],
        out_specs=pl.BlockSpec((tm, n), lambda i: (i, 0)),
        out_shape=jax.ShapeDtypeStruct((rows, n), BF16),
        compiler_params=pltpu.CompilerParams(
            dimension_semantics=("arbitrary",), vmem_limit_bytes=VMEM_LIMIT),
        name="memkv",
    )(mem2, g, w)


def _tail_kernel(x_ref, d_ref, f_ref, wo_ref, g_ref, wq_ref, kv_ref, wco_ref,
                 gm_ref, wu_ref, wd_ref, gf_ref, o_ref, *, fc):
    half = d_ref.shape[1]
    h1 = (x_ref[...]
          + jnp.dot(d_ref[...], wo_ref[:half, :], preferred_element_type=F32)
          + jnp.dot(f_ref[...], wo_ref[half:, :], preferred_element_type=F32))
    uq = _rms(h1, g_ref[...], EPS).astype(BF16)
    dm = h1.shape[1]
    hd = dm // N_CROSS_HEADS
    cq = (jnp.dot(uq, wq_ref[...], preferred_element_type=F32) * (hd ** -0.5)).astype(BF16)
    outs = []
    for hh in range(N_CROSS_HEADS):
        qh = cq[:, hh * hd:(hh + 1) * hd]
        kh = kv_ref[:, hh * hd:(hh + 1) * hd]
        vh = kv_ref[:, dm + hh * hd:dm + (hh + 1) * hd]
        s = lax.dot_general(qh, kh, (((1,), (1,)), ((), ())), preferred_element_type=F32)
        p = jnp.exp(s - jnp.max(s, axis=1, keepdims=True))
        l = jnp.sum(p, axis=1, keepdims=True)
        oh = jnp.dot(p.astype(BF16), vh, preferred_element_type=F32) / l
        outs.append(oh.astype(BF16))
    co = jnp.concatenate(outs, axis=1)
    h2 = h1 + jnp.dot(co, wco_ref[...], preferred_element_type=F32)

    u = _rms(h2, gm_ref[...], EPS).astype(BF16)
    acc = h2
    for c in range(wu_ref.shape[1] // fc):
        z = jnp.dot(u, wu_ref[:, c * fc:(c + 1) * fc], preferred_element_type=F32)
        a = jnp.square(jnp.maximum(z, 0.0)).astype(BF16)
        acc = acc + jnp.dot(a, wd_ref[c * fc:(c + 1) * fc, :], preferred_element_type=F32)
    o_ref[...] = _rms(acc, gf_ref[...], EPS)


def _tail(x2, d2, f2, w_out, g_cross, w_cq, kv, w_co, g_mlp, w_up, w_down, g_final, *, seq, tm, fc):
    rows, dm = x2.shape
    tiles_per_batch = seq // tm
    n_mem = kv.shape[1]
    const = lambda i: (0, 0)
    resident = lambda w: pl.BlockSpec(w.shape, const, pipeline_mode=pl.Buffered(1))
    return pl.pallas_call(
        functools.partial(_tail_kernel, fc=fc),
        grid=(rows // tm,),
        in_specs=[
            pl.BlockSpec((tm, dm), lambda i: (i, 0)),
            pl.BlockSpec((tm, d2.shape[1]), lambda i: (i, 0)),
            pl.BlockSpec((tm, f2.shape[1]), lambda i: (i, 0)),
            resident(w_out),
            pl.BlockSpec((1, dm), const),
            resident(w_cq),
            pl.BlockSpec((None, n_mem, kv.shape[2]), lambda i: (i // tiles_per_batch, 0, 0)),
            resident(w_co),
            pl.BlockSpec((1, dm), const),
            resident(w_up),
            resident(w_down),
            pl.BlockSpec((1, dm), const),
        ],
        out_specs=pl.BlockSpec((tm, dm), lambda i: (i, 0)),
        out_shape=jax.ShapeDtypeStruct((rows, dm), F32),
        compiler_params=pltpu.CompilerParams(
            dimension_semantics=("arbitrary",), vmem_limit_bytes=VMEM_LIMIT),
        name="tail",
    )(x2, d2, f2, w_out, g_cross, w_cq, kv, w_co, g_mlp, w_up, w_down, g_final)


def _rope_tables(seq):
    pos = jnp.arange(seq, dtype=F32)
    inv_freq = ROPE_THETA ** (-jnp.arange(0, ROT_DIM, 2, dtype=F32) / ROT_DIM)
    ang = pos[:, None] * inv_freq[None, :]
    cos, sin = jnp.cos(ang), jnp.sin(ang)
    lane = jnp.arange(LANES)
    l64 = lane % QK_DIM
    fidx = l64 % (ROT_DIM // 2)
    in_rot = (l64 < ROT_DIM)[None, :]
    first = (l64 < ROT_DIM // 2)[None, :]
    rc = jnp.where(in_rot, cos[:, fidx], 1.0)
    ra = jnp.where(in_rot & ~first, sin[:, fidx], 0.0)
    rb = jnp.where(first, -sin[:, fidx], 0.0)
    return (rc.astype(F32), ra.astype(F32), rb.astype(F32)), (cos.T.astype(F32), sin.T.astype(F32))


def _gate_lane_layout(cols):
    odd = jnp.repeat(cols[..., 1::2], SUBLANES, axis=-1)
    even = jnp.repeat(cols[..., 0::2], SUBLANES, axis=-1)
    pad = jnp.zeros(cols.shape[:-1] + (QK_DIM - odd.shape[-1],), cols.dtype)
    return jnp.concatenate([odd, pad, even, pad], axis=-1)


def kernel(x, mem, norm_mix_g, w_in, b_forget, lam_q1, lam_k1, lam_q2, lam_k2,
           diff_subln_g, fox_out_g, w_out, norm_cross_g, norm_mem_g, w_cq, w_ckv, w_co,
           norm_mlp_g, w_up, w_down, norm_final_g):
    batch, seq, dm = x.shape
    n_mem = mem.shape[1]
    assert norm_mix_g.shape[0] == 1, "single-layer block"
    assert w_in.shape[2] == 6 * 512 + N_FOX_HEADS

    x2 = x.reshape(batch * seq, dm)
    w = w_in[0]
    dq, dk, dv, fq, fk, fv = (w[:, c * 512:(c + 1) * 512] for c in range(6))
    w_nat = jnp.concatenate([dk, fk, _gate_lane_layout(w[:, 6 * 512:])], axis=1).astype(BF16)
    w_tr = jnp.concatenate([dq, fq, dv, fv], axis=1).T.astype(BF16)
    b_gate = _gate_lane_layout(b_forget[0]).reshape(1, LANES)
    rope_nat, rope_tr = _rope_tables(seq)

    qt, kn, vt = _inproj(x2, norm_mix_g[0].reshape(1, dm), w_nat, w_tr, b_gate,
                         rope_nat, rope_tr, batch=batch, seq=seq, tm=ROW_TILE, tk=ATTN_TK)

    lam4 = jnp.stack([lam_q1[0], lam_k1[0], lam_q2[0], lam_k2[0]], axis=0)
    d_out = _attention("diff", qt, kn, vt, (lam4, diff_subln_g[0].reshape(LANES, 1)),
                       tq=ATTN_TQ, tk=ATTN_TK)
    f_out = _attention("fox", qt, kn, vt, (jnp.tile(fox_out_g[0], 2).reshape(LANES, 1),),
                       tq=ATTN_TQ, tk=ATTN_TK)

    kv = _memkv(mem.reshape(batch * n_mem, dm), norm_mem_g[0].reshape(1, dm),
                w_ckv[0].astype(BF16), tm=ROW_TILE)
    out = _tail(x2, d_out.reshape(batch * seq, -1), f_out.reshape(batch * seq, -1),
                w_out[0].astype(BF16), norm_cross_g[0].reshape(1, dm), w_cq[0].astype(BF16),
                kv.reshape(batch, n_mem, -1), w_co[0].astype(BF16),
                norm_mlp_g[0].reshape(1, dm), w_up[0].astype(BF16), w_down[0].astype(BF16),
                norm_final_g.reshape(1, dm), seq=seq, tm=ROW_TILE, fc=MLP_FF_CHUNK)
    return out.reshape(batch, seq, dm)
```
